```python
import math
import jax, jax.numpy as jnp
from jax import lax
import numpy as np

D_MODEL = 1024
BATCH = 8
SEQ = 8192
DEPTH = 2
DEC_BATCH = 4
DEC_SEQ = 8192
PAST_LEN = 128

GROUP_WIDTH = 64
D_HYENA = 384
D_CONF = 256
D_RG = 384
D_MIX = D_HYENA + D_CONF + D_RG
RG_HEADS = D_RG // GROUP_WIDTH
HYENA_SHORT_K = 3
HYENA_EMB = 33
HYENA_BANDS = (HYENA_EMB - 1) // 2
HYENA_FILTER_ORDER = 64
HYENA_TARGET = 1e-2
HYENA_FAST_DECAY_PCT = 0.3
HYENA_SLOW_DECAY_PCT = 1.5
CONF_K = 31
RG_CONV_K = 4
RG_C = 8.0
IN_COLS = 3 * D_HYENA + 2 * D_CONF + D_RG + D_MIX
EPS = 1e-6

kernel_name = "hybrid_hyena_conformer_rglru_encoder"


def rmsnorm(x, g):
    xf = x.astype(jnp.float32)
    y = xf * lax.rsqrt(jnp.mean(xf * xf, axis=-1, keepdims=True) + EPS)
    return (y * g.astype(jnp.float32)).astype(x.dtype)


def layernorm(x, g, b):
    xf = x.astype(jnp.float32)
    mu = jnp.mean(xf, axis=-1, keepdims=True)
    xc = xf - mu
    var = jnp.mean(xc * xc, axis=-1, keepdims=True)
    y = xc * lax.rsqrt(var + EPS) * g.astype(jnp.float32) + b.astype(jnp.float32)
    return y.astype(x.dtype)


def dwconv(x, w, b, pad_left, pad_right):
    C = x.shape[-1]
    y = lax.conv_general_dilated(
        x, w[:, None, :].astype(x.dtype), window_strides=(1,),
        padding=[(pad_left, pad_right)],
        dimension_numbers=("NWC", "WIO", "NWC"), feature_group_count=C)
    return y + b.astype(x.dtype)


def hyena_filters(L, w1, b1, w2, b2, w3, b3, w4, freq):
    f32 = jnp.float32
    t = jnp.linspace(0.0, 1.0, L, dtype=f32)[:, None]
    w = (2.0 * math.pi / L) * jnp.arange(L, dtype=f32)[:, None]
    f = jnp.linspace(1e-4, HYENA_BANDS - 1, HYENA_BANDS, dtype=f32)[None]
    z = jnp.concatenate([t, jnp.cos(f * w), -jnp.sin(f * w)], axis=-1)
    fr = freq.astype(f32)
    hdn = jnp.sin(fr * (z @ w1.astype(f32) + b1.astype(f32)))
    hdn = jnp.sin(fr * (hdn @ w2.astype(f32) + b2.astype(f32)))
    hdn = jnp.sin(fr * (hdn @ w3.astype(f32) + b3.astype(f32)))
    h = hdn @ w4.astype(f32)
    max_decay = math.log(HYENA_TARGET) / HYENA_FAST_DECAY_PCT
    min_decay = math.log(HYENA_TARGET) / HYENA_SLOW_DECAY_PCT
    deltas = jnp.abs(jnp.linspace(min_decay, max_decay, D_HYENA, dtype=f32))
    decay = jnp.exp(-t * deltas[None])
    return h[:, :D_HYENA] * decay, h[:, D_HYENA:] * decay


def two_sided_fftconv(u, h_fwd, h_bwd):
    L, C = h_fwd.shape
    k = jnp.concatenate([h_fwd, jnp.zeros((1, C), jnp.float32), h_bwd[1:][::-1]], axis=0)
    u_f = jnp.fft.rfft(u, n=2 * L, axis=1)
    k_f = jnp.fft.rfft(k, n=2 * L, axis=0)
    return jnp.fft.irfft(u_f * k_f[None], n=2 * L, axis=1)[:, :L]


def hyena_branch(p, conv_w, conv_b, w1, b1, w2, b2, w3, b3, w4, freq, d_skip):
    L = p.shape[1]
    uc = dwconv(p, conv_w, conv_b, 1, 1)
    x0, x1, v = uc[..., :D_HYENA], uc[..., D_HYENA:2 * D_HYENA], uc[..., 2 * D_HYENA:]
    v = (v * x1).astype(jnp.float32)
    h_fwd, h_bwd = hyena_filters(L, w1, b1, w2, b2, w3, b3, w4, freq)
    y = two_sided_fftconv(v, h_fwd, h_bwd) + v * d_skip.astype(jnp.float32)
    return (y * x0.astype(jnp.float32)).astype(p.dtype)


def conformer_branch(p, dw_w, dw_b, ln_g, ln_b, pw_w, pw_b):
    a, b = p[..., :D_CONF], p[..., D_CONF:]
    u = a * jax.nn.sigmoid(b)
    half = (CONF_K - 1) // 2
    u = dwconv(u, dw_w, dw_b, half, half)
    u = jax.nn.silu(layernorm(u, ln_g, ln_b))
    return jnp.einsum('blc,ce->ble', u, pw_w) + pw_b


def linear_scan(a, b, reverse):
    def comb(e1, e2):
        a1, b1 = e1
        a2, b2 = e2
        return a1 * a2, a2 * b1 + b2
    return lax.associative_scan(comb, (a, b), reverse=reverse, axis=1)[1]


def rglru_direction(xr, wa, ba, wx, bx, lam, reverse):
    B, L, _ = xr.shape
    xb = xr.reshape(B, L, RG_HEADS, GROUP_WIDTH)
    gate_a = jnp.einsum('blhi,hij->blhj', xb, wa).reshape(B, L, D_RG) + ba
    gate_x = jnp.einsum('blhi,hij->blhj', xb, wx).reshape(B, L, D_RG) + bx
    r = jax.nn.sigmoid(gate_a.astype(jnp.float32))
    i = jax.nn.sigmoid(gate_x.astype(jnp.float32))
    log_a = -RG_C * r * jax.nn.softplus(-lam.astype(jnp.float32))
    a = jnp.exp(log_a)
    mult = jnp.sqrt(-jnp.expm1(2.0 * log_a))
    bterm = mult * (i * xr.astype(jnp.float32))
    return linear_scan(a, bterm, reverse)


def rglru_branch(p, conv_w, conv_b, wa, ba, wx, bx, lam):
    xr = dwconv(p, conv_w, conv_b, RG_CONV_K // 2, RG_CONV_K - 1 - RG_CONV_K // 2)
    h_f = rglru_direction(xr, wa[0], ba[0], wx[0], bx[0], lam[0], False)
    h_b = rglru_direction(xr, wa[1], ba[1], wx[1], bx[1], lam[1], True)
    return (h_f + h_b).astype(p.dtype)


def mixer_layer(x, norm_g, w_in, hy_conv_w, hy_conv_b, hy_w1, hy_b1, hy_w2, hy_b2, hy_w3, hy_b3,
                hy_w4, hy_freq, hy_d, cf_dw_w, cf_dw_b, cf_ln_g, cf_ln_b, cf_pw_w, cf_pw_b,
                rg_conv_w, rg_conv_b, rg_wa, rg_ba, rg_wx, rg_bx, rg_lam, grp_g, w_out):
    h = rmsnorm(x, norm_g)
    p = jnp.einsum('bld,de->ble', h, w_in)
    s1 = 3 * D_HYENA
    s2 = s1 + 2 * D_CONF
    s3 = s2 + D_RG
    p_hy, p_cf, p_rg, gate = p[..., :s1], p[..., s1:s2], p[..., s2:s3], p[..., s3:]
    y_hy = hyena_branch(p_hy, hy_conv_w, hy_conv_b, hy_w1, hy_b1, hy_w2, hy_b2, hy_w3, hy_b3,
                        hy_w4, hy_freq, hy_d)
    y_cf = conformer_branch(p_cf, cf_dw_w, cf_dw_b, cf_ln_g, cf_ln_b, cf_pw_w, cf_pw_b)
    y_rg = rglru_branch(p_rg, rg_conv_w, rg_conv_b, rg_wa, rg_ba, rg_wx, rg_bx, rg_lam)
    y = jnp.concatenate([
        rmsnorm(y_hy, grp_g[:D_HYENA]),
        rmsnorm(y_cf, grp_g[D_HYENA:D_HYENA + D_CONF]),
        rmsnorm(y_rg, grp_g[D_HYENA + D_CONF:])], axis=-1)
    y = y * jax.nn.silu(gate)
    return jnp.einsum('ble,ed->bld', y, w_out)


def trunk(x, layer_params, final_g):
    for l in range(DEPTH):
        x = x + mixer_layer(x, *[w[l] for w in layer_params])
    return rmsnorm(x, final_g)


def setup_inputs(seed: int = 0) -> dict:
    key = jax.random.key(seed)
    ks = jax.random.split(key, 32)
    f32 = jnp.float32

    def nrm(k, shape, scale):
        return jax.random.normal(k, shape, f32) * scale

    u = jax.random.uniform(ks[28], (DEPTH, 2, D_RG), f32, minval=0.9, maxval=0.999)
    a0 = u ** (1.0 / RG_C)
    rg_lam = jnp.log(a0) - jnp.log1p(-a0)
    return {
        "x_prompt": nrm(ks[0], (BATCH, SEQ, D_MODEL), 1.0),
        "x_sample": nrm(ks[1], (DEC_BATCH, DEC_SEQ, D_MODEL), 1.0),
        "norm_g": 1.0 + nrm(ks[2], (DEPTH, D_MODEL), 0.02),
        "w_in": nrm(ks[3], (DEPTH, D_MODEL, IN_COLS), D_MODEL ** -0.5),
        "hy_conv_w": nrm(ks[4], (DEPTH, HYENA_SHORT_K, 3 * D_HYENA), HYENA_SHORT_K ** -0.5),
        "hy_conv_b": nrm(ks[5], (DEPTH, 3 * D_HYENA), 0.02),
        "hy_w1": nrm(ks[6], (DEPTH, HYENA_EMB, HYENA_FILTER_ORDER), HYENA_EMB ** -0.5),
        "hy_b1": nrm(ks[7], (DEPTH, HYENA_FILTER_ORDER), 0.02),
        "hy_w2": nrm(ks[8], (DEPTH, HYENA_FILTER_ORDER, HYENA_FILTER_ORDER), HYENA_FILTER_ORDER ** -0.5),
        "hy_b2": nrm(ks[9], (DEPTH, HYENA_FILTER_ORDER), 0.02),
        "hy_w3": nrm(ks[10], (DEPTH, HYENA_FILTER_ORDER, HYENA_FILTER_ORDER), HYENA_FILTER_ORDER ** -0.5),
        "hy_b3": nrm(ks[11], (DEPTH, HYENA_FILTER_ORDER), 0.02),
        "hy_w4": nrm(ks[12], (DEPTH, HYENA_FILTER_ORDER, 2 * D_HYENA), HYENA_FILTER_ORDER ** -0.5),
        "hy_freq": 1.0 + nrm(ks[13], (DEPTH, HYENA_FILTER_ORDER), 0.02),
        "hy_d": nrm(ks[14], (DEPTH, D_HYENA), 0.1),
        "cf_dw_w": nrm(ks[15], (DEPTH, CONF_K, D_CONF), CONF_K ** -0.5),
        "cf_dw_b": nrm(ks[16], (DEPTH, D_CONF), 0.02),
        "cf_ln_g": 1.0 + nrm(ks[17], (DEPTH, D_CONF), 0.02),
        "cf_ln_b": nrm(ks[18], (DEPTH, D_CONF), 0.02),
        "cf_pw_w": nrm(ks[19], (DEPTH, D_CONF, D_CONF), D_CONF ** -0.5),
        "cf_pw_b": nrm(ks[20], (DEPTH, D_CONF), 0.02),
        "rg_conv_w": nrm(ks[21], (DEPTH, RG_CONV_K, D_RG), RG_CONV_K ** -0.5),
        "rg_conv_b": nrm(ks[22], (DEPTH, D_RG), 0.02),
        "rg_wa": nrm(ks[23], (DEPTH, 2, RG_HEADS, GROUP_WIDTH, GROUP_WIDTH), GROUP_WIDTH ** -0.5),
        "rg_ba": nrm(ks[24], (DEPTH, 2, D_RG), 0.02),
        "rg_wx": nrm(ks[25], (DEPTH, 2, RG_HEADS, GROUP_WIDTH, GROUP_WIDTH), GROUP_WIDTH ** -0.5),
        "rg_bx": nrm(ks[26], (DEPTH, 2, D_RG), 0.02),
        "rg_lam": rg_lam,
        "grp_g": 1.0 + nrm(ks[27], (DEPTH, D_MIX), 0.02),
        "w_out": nrm(ks[29], (DEPTH, D_MIX, D_MODEL), D_MIX ** -0.5),
        "final_g": 1.0 + nrm(ks[30], (D_MODEL,), 0.02),
    }


def reference(x_prompt, x_sample, norm_g, w_in, hy_conv_w, hy_conv_b, hy_w1, hy_b1, hy_w2, hy_b2,
              hy_w3, hy_b3, hy_w4, hy_freq, hy_d, cf_dw_w, cf_dw_b, cf_ln_g, cf_ln_b, cf_pw_w,
              cf_pw_b, rg_conv_w, rg_conv_b, rg_wa, rg_ba, rg_wx, rg_bx, rg_lam, grp_g, w_out,
              final_g):
    layer_params = (norm_g, w_in, hy_conv_w, hy_conv_b, hy_w1, hy_b1, hy_w2, hy_b2, hy_w3, hy_b3,
                    hy_w4, hy_freq, hy_d, cf_dw_w, cf_dw_b, cf_ln_g, cf_ln_b, cf_pw_w, cf_pw_b,
                    rg_conv_w, rg_conv_b, rg_wa, rg_ba, rg_wx, rg_bx, rg_lam, grp_g, w_out)
    y_prompt = trunk(x_prompt, layer_params, final_g)
    y_sample = trunk(x_sample, layer_params, final_g)
    return (y_prompt, y_sample)
```

```python
import functools
import math

import jax
import jax.numpy as jnp
from jax import lax
from jax.experimental import pallas as pl
from jax.experimental.pallas import tpu as pltpu

F32 = jnp.float32
BF16 = jnp.bfloat16

D_MODEL = 1024
D_HYENA = 384
D_CONF = 256
D_RG = 384
D_MIX = D_HYENA + D_CONF + D_RG
GROUP_WIDTH = 64
HYENA_BANDS = 16
HYENA_ORDER = 64
HYENA_TARGET = 1e-2
HYENA_FAST_DECAY_PCT = 0.3
HYENA_SLOW_DECAY_PCT = 1.5
CONF_K = 31
RG_C = 8.0
EPS = 1e-6

LANES = 128
SUBLANES = 8
MXU_DIM = 256
VMEM_LIMIT_BYTES = 56 * 1024 * 1024

TOEPLITZ_BLOCK = MXU_DIM
BATCH_PAD = 16
FILTER_TILE = 2048
PROJ_ROWS = 512
CONF_ROWS = 512
CONF_HALO = 16
CONF_SUB = 64
RG_CHUNK = 256
RG_PAD = 8


def _params(*sem):
    return pltpu.CompilerParams(dimension_semantics=sem, vmem_limit_bytes=VMEM_LIMIT_BYTES)


def _rms(x, g):
    return x * lax.rsqrt(jnp.mean(x * x, axis=-1, keepdims=True) + EPS) * g


def _filters_kernel(w1t_ref, w1c_ref, w1s_ref, b1_ref, w2_ref, b2_ref, w3_ref, b3_ref, w4_ref,
                    fr_ref, o_ref, *, seq_len):
    i = pl.program_id(0)
    tile = o_ref.shape[1]
    m = i * tile + lax.broadcasted_iota(jnp.int32, (1, tile), 1)
    pos = jnp.abs(m - seq_len).astype(F32)
    t = pos / (seq_len - 1.0)
    w = (2.0 * math.pi / seq_len) * pos
    band = lax.broadcasted_iota(jnp.int32, (HYENA_BANDS, 1), 0).astype(F32)
    f = 1e-4 + band * ((HYENA_BANDS - 1 - 1e-4) / (HYENA_BANDS - 1))
    fw = f * w
    hi = lax.Precision.HIGHEST
    fr = fr_ref[...]
    pre = (w1t_ref[...] * t + jnp.dot(w1c_ref[...], jnp.cos(fw), precision=hi)
           - jnp.dot(w1s_ref[...], jnp.sin(fw), precision=hi))
    h = jnp.sin(fr * (pre + b1_ref[...]))
    h = jnp.sin(fr * (jnp.dot(w2_ref[...], h, precision=hi) + b2_ref[...]))
    h = jnp.sin(fr * (jnp.dot(w3_ref[...], h, precision=hi) + b3_ref[...]))
    k = jnp.dot(w4_ref[...], h, precision=hi)
    max_decay = math.log(HYENA_TARGET) / HYENA_FAST_DECAY_PCT
    min_decay = math.log(HYENA_TARGET) / HYENA_SLOW_DECAY_PCT
    ch = lax.broadcasted_iota(jnp.int32, (D_HYENA, 1), 0).astype(F32)
    delta = jnp.abs(min_decay + ch * ((max_decay - min_decay) / (D_HYENA - 1)))
    decay = jnp.exp(-t * delta)
    k = jnp.where(m >= seq_len, k[:D_HYENA], k[D_HYENA:]) * decay
    o_ref[...] = jnp.where(m == 0, 0.0, k)


def _filters(w1, b1, w2, b2, w3, b3, w4, freq, seq_len):
    w1t = w1.T
    col = lambda v: v.reshape(-1, 1)
    args = (w1t[:, 0:1], w1t[:, 1:1 + HYENA_BANDS], w1t[:, 1 + HYENA_BANDS:], col(b1), w2.T, col(b2),
            w3.T, col(b3), w4.T, col(freq))
    full = lambda a: pl.BlockSpec(a.shape, lambda i: (0, 0))
    return pl.pallas_call(
        functools.partial(_filters_kernel, seq_len=seq_len),
        grid=(2 * seq_len // FILTER_TILE,),
        in_specs=[full(a) for a in args],
        out_specs=pl.BlockSpec((D_HYENA, FILTER_TILE), lambda i: (0, i)),
        out_shape=jax.ShapeDtypeStruct((D_HYENA, 2 * seq_len), F32),
        compiler_params=_params("arbitrary"),
        name="filters",
    )(*args)


def _hy_proj_kernel(x_ref, xp_ref, xn_ref, g_ref, w_ref, cw_ref, u_ref, x0_ref, *, n_seq):
    j = pl.program_id(0)
    b = pl.program_id(1)
    tb = x_ref.shape[1]
    g = g_ref[...]
    nt = (((1,), (1,)), ((), ()))
    xn = _rms(x_ref[0], g).astype(BF16)
    p = lax.dot_general(w_ref[...], xn, nt, preferred_element_type=F32)
    halo = jnp.concatenate([xp_ref[0], xn_ref[0]], axis=0)
    ph = lax.dot_general(w_ref[...], _rms(halo, g).astype(BF16), nt, preferred_element_type=F32)
    prev = jnp.where(j > 0, ph[:, SUBLANES - 1:SUBLANES], 0.0)
    nxt = jnp.where(j < pl.num_programs(0) - 1, ph[:, SUBLANES:SUBLANES + 1], 0.0)
    lane = lax.broadcasted_iota(jnp.int32, (1, tb), 1)
    pm1 = jnp.where(lane == 0, prev, pltpu.roll(p, 1, 1))
    pp1 = jnp.where(lane == tb - 1, nxt, pltpu.roll(p, tb - 1, 1))
    cw = cw_ref[...]
    uc = cw[:, 0:1] * pm1 + cw[:, 1:2] * p + cw[:, 2:3] * pp1 + cw[:, 3:4]
    x0 = uc[:D_HYENA]
    u = uc[2 * D_HYENA:] * uc[D_HYENA:2 * D_HYENA]

    @pl.when(b == 0)
    def _():
        zeros = jnp.zeros((D_HYENA, BATCH_PAD - n_seq, tb), F32)
        u_ref[:, 0, n_seq:, :] = zeros
        x0_ref[:, 0, n_seq:, :] = zeros

    u_ref[:, 0, pl.ds(b, 1), :] = u[:, None, :]
    x0_ref[:, 0, pl.ds(b, 1), :] = x0[:, None, :]


def _hy_proj(x, norm_g, w_hy_t, cw):
    n_seq, seq_len, _ = x.shape
    tb = TOEPLITZ_BLOCK
    nj = seq_len // tb
    per8 = tb // SUBLANES
    last8 = seq_len // SUBLANES - 1
    out = jax.ShapeDtypeStruct((D_HYENA, nj, BATCH_PAD, tb), F32)
    out_spec = pl.BlockSpec((D_HYENA, 1, BATCH_PAD, tb), lambda j, b: (0, j, 0, 0))
    return pl.pallas_call(
        functools.partial(_hy_proj_kernel, n_seq=n_seq),
        grid=(nj, n_seq),
        in_specs=[
            pl.BlockSpec((1, tb, D_MODEL), lambda j, b: (b, j, 0)),
            pl.BlockSpec((1, SUBLANES, D_MODEL), lambda j, b: (b, jnp.maximum(j * per8 - 1, 0), 0)),
            pl.BlockSpec((1, SUBLANES, D_MODEL), lambda j, b: (b, jnp.minimum((j + 1) * per8, last8), 0)),
            pl.BlockSpec((1, D_MODEL), lambda j, b: (0, 0)),
            pl.BlockSpec((3 * D_HYENA, D_MODEL), lambda j, b: (0, 0)),
            pl.BlockSpec((3 * D_HYENA, SUBLANES), lambda j, b: (0, 0)),
        ],
        out_specs=[out_spec, out_spec],
        out_shape=[out, out],
        compiler_params=_params("arbitrary", "arbitrary"),
        name="hy_proj",
    )(x, x, x, norm_g, w_hy_t, cw)


def _longconv_kernel(d_ref, k_ref, u_ref, x0_ref, o_ref, g_ref, ub_ref, y_ref):
    c = pl.program_id(0)
    tb = TOEPLITZ_BLOCK
    half = tb // 2
    rows = u_ref.shape[1]
    nj = rows // BATCH_PAD
    seq_len = nj * tb
    nblk = 2 * seq_len // half

    tri = (lax.broadcasted_iota(jnp.int32, (half, half), 1) >= lax.broadcasted_iota(jnp.int32, (half, half), 0))

    def build(q, prev):
        off = pl.multiple_of(q * half, half)
        seg = jnp.broadcast_to(k_ref[0, :, pl.ds(off, half)], (half, half))
        cur = pltpu.roll(seg, 0, 1, stride=1, stride_axis=0)
        blk = jnp.where(tri, cur, prev).astype(BF16)
        g_ref[0:half, pl.ds(off, half)] = blk

        @pl.when(q < nblk - 1)
        def _():
            g_ref[half:tb, pl.ds(pl.multiple_of(off + half, half), half)] = blk

        return cur

    lax.fori_loop(0, nblk, build, jnp.zeros((half, half), F32))

    ub_ref[...] = u_ref[0].astype(BF16)

    def tile(d):
        return g_ref[:, seq_len + tb * d:seq_len + tb * (d + 1)]

    y_ref[...] = jnp.dot(ub_ref[...], tile(0), preferred_element_type=F32)
    for d in range(1, nj):
        n = BATCH_PAD * (nj - d)
        y_ref[BATCH_PAD * d:, :] += jnp.dot(ub_ref[0:n, :], tile(d), preferred_element_type=F32)
        y_ref[0:n, :] += jnp.dot(ub_ref[BATCH_PAD * d:, :], tile(-d), preferred_element_type=F32)

    o_ref[0] = (y_ref[...] + u_ref[0] * d_ref[c]) * x0_ref[0]


def _longconv(kfull, u, x0, d_skip):
    n_ch, nj, bp, tb = u.shape
    rows = nj * bp
    seq_len = nj * tb
    blk = pl.BlockSpec((1, rows, tb), lambda c: (c, 0, 0))
    out = pl.pallas_call(
        _longconv_kernel,
        grid=(n_ch,),
        in_specs=[
            pl.BlockSpec(memory_space=pltpu.SMEM),
            pl.BlockSpec((1, 1, 2 * seq_len), lambda c: (c, 0, 0)),
            blk, blk,
        ],
        out_specs=blk,
        out_shape=jax.ShapeDtypeStruct((n_ch, rows, tb), F32),
        scratch_shapes=[
            pltpu.VMEM((tb, 2 * seq_len), BF16),
            pltpu.VMEM((rows, tb), BF16),
            pltpu.VMEM((rows, tb), F32),
        ],
        compiler_params=_params("arbitrary"),
        name="longconv",
    )(d_skip, kfull.reshape(n_ch, 1, 2 * seq_len), u.reshape(n_ch, rows, tb), x0.reshape(n_ch, rows, tb))
    return out.reshape(n_ch, nj, bp, tb)


def _in_proj_kernel(x_ref, g_ref, w_ref, cf_ref, rg_ref, gate_ref):
    xn = _rms(x_ref[...], g_ref[...]).astype(BF16)
    p = jnp.dot(xn, w_ref[...], preferred_element_type=F32)
    cf_ref[...] = p[:, :2 * D_CONF]
    rg_ref[...] = p[:, 2 * D_CONF:2 * D_CONF + D_RG]
    gate_ref[...] = p[:, 2 * D_CONF + D_RG:]


def _in_proj(x2d, norm_g, w_rest):
    n_tok = x2d.shape[0]
    n_col = w_rest.shape[1]
    row = lambda w: pl.BlockSpec((PROJ_ROWS, w), lambda i: (i, 0))
    return pl.pallas_call(
        _in_proj_kernel,
        grid=(n_tok // PROJ_ROWS,),
        in_specs=[row(D_MODEL), pl.BlockSpec((1, D_MODEL), lambda i: (0, 0)),
                  pl.BlockSpec((D_MODEL, n_col), lambda i: (0, 0))],
        out_specs=[row(2 * D_CONF), row(D_RG), row(D_MIX)],
        out_shape=[jax.ShapeDtypeStruct((n_tok, w), F32) for w in (2 * D_CONF, D_RG, D_MIX)],
        compiler_params=_params("arbitrary"),
        name="in_proj",
    )(x2d, norm_g, w_rest)


def _glu(p):
    return p[:, :D_CONF] * jax.nn.sigmoid(p[:, D_CONF:])


def _conformer_kernel(p_ref, pp_ref, pn_ref, dw_ref, vec_ref, pw_ref, o_ref, ext_ref):
    t = pl.program_id(1)
    rows = p_ref.shape[1]
    half = (CONF_K - 1) // 2
    ext_ref[0:CONF_HALO, :] = jnp.where(t > 0, _glu(pp_ref[0]), 0.0)
    ext_ref[CONF_HALO:CONF_HALO + rows, :] = _glu(p_ref[0])
    ext_ref[CONF_HALO + rows:, :] = jnp.where(t < pl.num_programs(1) - 1, _glu(pn_ref[0]), 0.0)
    vec = vec_ref[...]
    dw_b, ln_g, ln_b, pw_b = vec[0:1], vec[1:2], vec[2:3], vec[3:4]
    dw = dw_ref[...]
    for s in range(rows // CONF_SUB):
        base = CONF_HALO - half + s * CONF_SUB
        acc = jnp.broadcast_to(dw_b, (CONF_SUB, D_CONF))
        for k in range(CONF_K):
            acc = acc + dw[k:k + 1] * ext_ref[base + k:base + k + CONF_SUB, :]
        mu = jnp.mean(acc, axis=-1, keepdims=True)
        xc = acc - mu
        var = jnp.mean(xc * xc, axis=-1, keepdims=True)
        y = xc * lax.rsqrt(var + EPS) * ln_g + ln_b
        y = y * jax.nn.sigmoid(y)
        o_ref[0, s * CONF_SUB:(s + 1) * CONF_SUB, :] = (
            jnp.dot(y.astype(BF16), pw_ref[...], preferred_element_type=F32) + pw_b)


def _conformer(p_cf, dw_w, vec, pw_w):
    n_seq, seq_len, _ = p_cf.shape
    per = CONF_ROWS // CONF_HALO
    last = seq_len // CONF_HALO - 1
    return pl.pallas_call(
        _conformer_kernel,
        grid=(n_seq, seq_len // CONF_ROWS),
        in_specs=[
            pl.BlockSpec((1, CONF_ROWS, 2 * D_CONF), lambda b, t: (b, t, 0)),
            pl.BlockSpec((1, CONF_HALO, 2 * D_CONF), lambda b, t: (b, jnp.maximum(t * per - 1, 0), 0)),
            pl.BlockSpec((1, CONF_HALO, 2 * D_CONF), lambda b, t: (b, jnp.minimum((t + 1) * per, last), 0)),
            pl.BlockSpec((CONF_K, D_CONF), lambda b, t: (0, 0)),
            pl.BlockSpec((SUBLANES, D_CONF), lambda b, t: (0, 0)),
            pl.BlockSpec((D_CONF, D_CONF), lambda b, t: (0, 0)),
        ],
        out_specs=pl.BlockSpec((1, CONF_ROWS, D_CONF), lambda b, t: (b, t, 0)),
        out_shape=jax.ShapeDtypeStruct((n_seq, seq_len, D_CONF), F32),
        scratch_shapes=[pltpu.VMEM((CONF_ROWS + 2 * CONF_HALO, D_CONF), F32)],
        compiler_params=_params("arbitrary", "arbitrary"),
        name="conformer",
    )(p_cf, p_cf, p_cf, dw_w, vec, pw_w)


def _shift_sub(x, s, fill, reverse):
    r = x.shape[1]
    sub = lax.broadcasted_iota(jnp.int32, (1, r, 1), 1)
    if reverse:
        return jnp.where(sub < r - s, pltpu.roll(x, r - s, 1), fill)
    return jnp.where(sub >= s, pltpu.roll(x, s, 1), fill)


def _shift_tile(x, k, fill, reverse):
    pad = jnp.full((k,) + x.shape[1:], fill, x.dtype)
    if reverse:
        return jnp.concatenate([x[k:], pad], axis=0)
    return jnp.concatenate([pad, x[:x.shape[0] - k]], axis=0)


def _scan_chunk(a, b, carry, reverse):
    rows, lanes = a.shape
    n = rows // SUBLANES
    a = a.reshape(n, SUBLANES, lanes)
    b = b.reshape(n, SUBLANES, lanes)
    s = 1
    while s < SUBLANES:
        b = b + a * _shift_sub(b, s, 0.0, reverse)
        a = a * _shift_sub(a, s, 1.0, reverse)
        s *= 2
    edge = 0 if reverse else SUBLANES - 1
    at = jnp.broadcast_to(a[:, edge:edge + 1, :], a.shape)
    bt = jnp.broadcast_to(b[:, edge:edge + 1, :], b.shape)
    k = 1
    while k < n:
        bt = bt + at * _shift_tile(bt, k, 0.0, reverse)
        at = at * _shift_tile(at, k, 1.0, reverse)
        k *= 2
    h_in = _shift_tile(bt, 1, 0.0, reverse) + _shift_tile(at, 1, 1.0, reverse) * carry
    return (b + a * h_in).reshape(rows, lanes)


def _rglru_kernel(p_ref, cw_ref, w_ref, vec_ref, o_ref, xp_ref, xr_ref):
    seq_len = p_ref.shape[1]
    lanes = p_ref.shape[2]
    n_chunk = seq_len // RG_CHUNK
    zeros = jnp.zeros((RG_PAD, lanes), F32)
    xp_ref[0:RG_PAD, :] = zeros
    xp_ref[RG_PAD + seq_len:, :] = zeros
    xp_ref[RG_PAD:RG_PAD + seq_len, :] = p_ref[0]
    cw = cw_ref[...]
    vec = vec_ref[0]
    conv_b = vec[0:1]

    def gates(xr, d):
        g = jnp.dot(xr.astype(BF16), w_ref[0, d], preferred_element_type=F32)
        r = jax.nn.sigmoid(g[:, :lanes] + vec[1 + 4 * d:2 + 4 * d])
        i = jax.nn.sigmoid(g[:, lanes:] + vec[2 + 4 * d:3 + 4 * d])
        sp = jax.nn.softplus(-vec[3 + 4 * d:4 + 4 * d])
        a = jnp.exp(-RG_C * r * sp)
        return a, jnp.sqrt(1.0 - a * a) * (i * xr)

    def fwd(c, carry):
        t0 = pl.multiple_of(c * RG_CHUNK, RG_CHUNK)
        x = xp_ref[pl.ds(t0, RG_CHUNK + 2 * RG_PAD), :]
        xr = conv_b
        for k in range(4):
            xr = xr + cw[k:k + 1] * x[RG_PAD - 2 + k:RG_PAD - 2 + k + RG_CHUNK]
        xr_ref[pl.ds(t0, RG_CHUNK), :] = xr
        a, bt = gates(xr, 0)
        h = _scan_chunk(a, bt, carry, False)
        o_ref[0, pl.ds(t0, RG_CHUNK), :] = h
        return h[RG_CHUNK - 1:RG_CHUNK]

    lax.fori_loop(0, n_chunk, fwd, jnp.zeros((1, lanes), F32))

    def bwd(i, carry):
        t0 = pl.multiple_of((n_chunk - 1 - i) * RG_CHUNK, RG_CHUNK)
        a, bt = gates(xr_ref[pl.ds(t0, RG_CHUNK), :], 1)
        h = _scan_chunk(a, bt, carry, True)
        o_ref[0, pl.ds(t0, RG_CHUNK), :] += h
        return h[0:1]

    lax.fori_loop(0, n_chunk, bwd, jnp.zeros((1, lanes), F32))


def _rglru(p_rg, conv_w, w_cat, vec):
    n_seq, seq_len, _ = p_rg.shape
    n_cb = D_RG // LANES
    blk = pl.BlockSpec((1, seq_len, LANES), lambda b, c: (b, 0, c))
    return pl.pallas_call(
        _rglru_kernel,
        grid=(n_seq, n_cb),
        in_specs=[
            blk,
            pl.BlockSpec((4, LANES), lambda b, c: (0, c)),
            pl.BlockSpec((1, 2, LANES, 2 * LANES), lambda b, c: (c, 0, 0, 0)),
            pl.BlockSpec((1, 2 * SUBLANES, LANES), lambda b, c: (c, 0, 0)),
        ],
        out_specs=blk,
        out_shape=jax.ShapeDtypeStruct((n_seq, seq_len, D_RG), F32),
        scratch_shapes=[pltpu.VMEM((seq_len + 2 * RG_PAD, LANES), F32), pltpu.VMEM((seq_len, LANES), F32)],
        compiler_params=_params("arbitrary", "arbitrary"),
        name="rglru",
    )(p_rg, conv_w, w_cat, vec)


def _out_proj_kernel(yh_ref, ycf_ref, yrg_ref, gate_ref, x_ref, gg_ref, w_ref, fg_ref, o_ref, *, final):
    b = pl.program_id(1)
    gg = gg_ref[...]
    yh = yh_ref[:, 0, pl.ds(b, 1), :][:, 0, :].T
    y = jnp.concatenate([
        _rms(yh, gg[:, :D_HYENA]),
        _rms(ycf_ref[0], gg[:, D_HYENA:D_HYENA + D_CONF]),
        _rms(yrg_ref[0], gg[:, D_HYENA + D_CONF:]),
    ], axis=-1)
    gate = gate_ref[0]
    y = y * (gate * jax.nn.sigmoid(gate))
    out = x_ref[0] + jnp.dot(y.astype(BF16), w_ref[...], preferred_element_type=F32)
    if final:
        out = _rms(out, fg_ref[...])
    o_ref[0] = out


def _out_proj(y_hy, y_cf, y_rg, gate, x, grp_g, w_out, final_g, final):
    n_seq, seq_len, _ = x.shape
    n_ch, nj, bp, tb = y_hy.shape
    tok = lambda w: pl.BlockSpec((1, tb, w), lambda j, b: (b, j, 0))
    vec = lambda w: pl.BlockSpec((1, w), lambda j, b: (0, 0))
    return pl.pallas_call(
        functools.partial(_out_proj_kernel, final=final),
        grid=(nj, n_seq),
        in_specs=[
            pl.BlockSpec((n_ch, 1, bp, tb), lambda j, b: (0, j, 0, 0)),
            tok(D_CONF), tok(D_RG), tok(D_MIX), tok(D_MODEL),
            vec(D_MIX), pl.BlockSpec((D_MIX, D_MODEL), lambda j, b: (0, 0)), vec(D_MODEL),
        ],
        out_specs=tok(D_MODEL),
        out_shape=jax.ShapeDtypeStruct((n_seq, seq_len, D_MODEL), F32),
        compiler_params=_params("arbitrary", "arbitrary"),
        name="out_proj",
    )(y_hy, y_cf, y_rg, gate, x, grp_g, w_out, final_g)


def _rg_gate_weights(wa, wx):
    per = LANES // GROUP_WIDTH
    n_cb = D_RG // LANES

    def diag(w):
        w = w.reshape(n_cb, per, GROUP_WIDTH, GROUP_WIDTH)
        eye = jnp.eye(per, dtype=w.dtype)
        return jnp.einsum("cpij,pq->cpiqj", w, eye).reshape(n_cb, LANES, LANES)

    dirs = [jnp.concatenate([diag(wa[d]), diag(wx[d])], axis=-1) for d in range(2)]
    return jnp.stack(dirs, axis=1).astype(BF16)


def _layer(x, final, final_g, norm_g, w_in, hy_conv_w, hy_conv_b, hy_w1, hy_b1, hy_w2, hy_b2, hy_w3, hy_b3,
           hy_w4, hy_freq, hy_d, cf_dw_w, cf_dw_b, cf_ln_g, cf_ln_b, cf_pw_w, cf_pw_b, rg_conv_w, rg_conv_b,
           rg_wa, rg_ba, rg_wx, rg_bx, rg_lam, grp_g, w_out):
    n_seq, seq_len, _ = x.shape
    s1 = 3 * D_HYENA
    g_row = norm_g.reshape(1, D_MODEL)

    kfull = _filters(hy_w1, hy_b1, hy_w2, hy_b2, hy_w3, hy_b3, hy_w4, hy_freq, seq_len)
    cw = jnp.concatenate([hy_conv_w.T, hy_conv_b[:, None], jnp.zeros((s1, SUBLANES - 4), F32)], axis=1)
    u, x0 = _hy_proj(x, g_row, w_in[:, :s1].T.astype(BF16), cw)
    y_hy = _longconv(kfull, u, x0, hy_d)

    p_cf, p_rg, gate = _in_proj(x.reshape(n_seq * seq_len, D_MODEL), g_row, w_in[:, s1:].astype(BF16))
    zrow = jnp.zeros((D_CONF,), F32)
    cf_vec = jnp.stack([cf_dw_b, cf_ln_g, cf_ln_b, cf_pw_b, zrow, zrow, zrow, zrow])
    y_cf = _conformer(p_cf.reshape(n_seq, seq_len, 2 * D_CONF), cf_dw_w, cf_vec, cf_pw_w.astype(BF16))

    n_cb = D_RG // LANES
    zr = jnp.zeros((D_RG,), F32)
    rg_rows = [rg_conv_b, rg_ba[0], rg_bx[0], rg_lam[0], zr, rg_ba[1], rg_bx[1], rg_lam[1]] + [zr] * 8
    rg_vec = jnp.stack(rg_rows).reshape(2 * SUBLANES, n_cb, LANES).transpose(1, 0, 2)
    y_rg = _rglru(p_rg.reshape(n_seq, seq_len, D_RG), rg_conv_w, _rg_gate_weights(rg_wa, rg_wx), rg_vec)

    return _out_proj(y_hy, y_cf, y_rg, gate.reshape(n_seq, seq_len, D_MIX), x, grp_g.reshape(1, D_MIX),
                     w_out.astype(BF16), final_g.reshape(1, D_MODEL), final)


def kernel(x_prompt, x_sample, norm_g, w_in, hy_conv_w, hy_conv_b, hy_w1, hy_b1, hy_w2, hy_b2, hy_w3, hy_b3, hy_w4, hy_freq, hy_d, cf_dw_w, cf_dw_b, cf_ln_g, cf_ln_b, cf_pw_w, cf_pw_b, rg_conv_w, rg_conv_b, rg_wa, rg_ba, rg_wx, rg_bx, rg_lam, grp_g, w_out, final_g):
    layer_params = (norm_g, w_in, hy_conv_w, hy_conv_b, hy_w1, hy_b1, hy_w2, hy_b2, hy_w3, hy_b3, hy_w4, hy_freq,
                    hy_d, cf_dw_w, cf_dw_b, cf_ln_g, cf_ln_b, cf_pw_w, cf_pw_b, rg_conv_w, rg_conv_b, rg_wa,
                    rg_ba, rg_wx, rg_bx, rg_lam, grp_g, w_out)
    assert x_prompt.shape[1:] == x_sample.shape[1:]
    n_prompt = x_prompt.shape[0]
    assert n_prompt + x_sample.shape[0] <= BATCH_PAD
    depth = norm_g.shape[0]
    x = jnp.concatenate([x_prompt, x_sample], axis=0)
    for l in range(depth):
        x = _layer(x, l == depth - 1, final_g, *[w[l] for w in layer_params])
    return (x[:n_prompt], x[n_prompt:])
```

```python
import functools
import math

import jax
import jax.numpy as jnp
from jax import lax
from jax.experimental import pallas as pl
from jax.experimental.pallas import tpu as pltpu

F32 = jnp.float32
BF16 = jnp.bfloat16

D_MODEL = 1024
D_HYENA = 384
D_CONF = 256
D_RG = 384
D_MIX = D_HYENA + D_CONF + D_RG
GROUP_WIDTH = 64
HYENA_BANDS = 16
HYENA_ORDER = 64
HYENA_TARGET = 1e-2
HYENA_FAST_DECAY_PCT = 0.3
HYENA_SLOW_DECAY_PCT = 1.5
CONF_K = 31
RG_C = 8.0
EPS = 1e-6

LANES = 128
SUBLANES = 8
MXU_DIM = 256
VMEM_LIMIT_BYTES = 56 * 1024 * 1024

TOEPLITZ_BLOCK = MXU_DIM
BATCH_PAD = 16
FILTER_TILE = 2048
PROJ_ROWS = 512
CONF_ROWS = 512
CONF_HALO = 16
CONF_SUB = 64
RG_CHUNK = 256
RG_PAD = 8


def _params(*sem):
    return pltpu.CompilerParams(dimension_semantics=sem, vmem_limit_bytes=VMEM_LIMIT_BYTES)


def _rms(x, g):
    return x * lax.rsqrt(jnp.mean(x * x, axis=-1, keepdims=True) + EPS) * g


def _filters_kernel(w1t_ref, w1c_ref, w1s_ref, b1_ref, w2_ref, b2_ref, w3_ref, b3_ref, w4_ref,
                    fr_ref, o_ref, *, seq_len):
    i = pl.program_id(0)
    tile = o_ref.shape[1]
    m = i * tile + lax.broadcasted_iota(jnp.int32, (1, tile), 1)
    pos = jnp.abs(m - seq_len).astype(F32)
    t = pos / (seq_len - 1.0)
    w = (2.0 * math.pi / seq_len) * pos
    band = lax.broadcasted_iota(jnp.int32, (HYENA_BANDS, 1), 0).astype(F32)
    f = 1e-4 + band * ((HYENA_BANDS - 1 - 1e-4) / (HYENA_BANDS - 1))
    fw = f * w
    hi = lax.Precision.HIGHEST
    fr = fr_ref[...]
    pre = (w1t_ref[...] * t + jnp.dot(w1c_ref[...], jnp.cos(fw), precision=hi)
           - jnp.dot(w1s_ref[...], jnp.sin(fw), precision=hi))
    h = jnp.sin(fr * (pre + b1_ref[...]))
    h = jnp.sin(fr * (jnp.dot(w2_ref[...], h, precision=hi) + b2_ref[...]))
    h = jnp.sin(fr * (jnp.dot(w3_ref[...], h, precision=hi) + b3_ref[...]))
    k = jnp.dot(w4_ref[...], h, precision=hi)
    max_decay = math.log(HYENA_TARGET) / HYENA_FAST_DECAY_PCT
    min_decay = math.log(HYENA_TARGET) / HYENA_SLOW_DECAY_PCT
    ch = lax.broadcasted_iota(jnp.int32, (D_HYENA, 1), 0).astype(F32)
    delta = jnp.abs(min_decay + ch * ((max_decay - min_decay) / (D_HYENA - 1)))
    decay = jnp.exp(-t * delta)
    k = jnp.where(m >= seq_len, k[:D_HYENA], k[D_HYENA:]) * decay
    o_ref[...] = jnp.where(m == 0, 0.0, k)


def _filters(w1, b1, w2, b2, w3, b3, w4, freq, seq_len):
    w1t = w1.T
    col = lambda v: v.reshape(-1, 1)
    args = (w1t[:, 0:1], w1t[:, 1:1 + HYENA_BANDS], w1t[:, 1 + HYENA_BANDS:], col(b1), w2.T, col(b2),
            w3.T, col(b3), w4.T, col(freq))
    full = lambda a: pl.BlockSpec(a.shape, lambda i: (0, 0))
    return pl.pallas_call(
        functools.partial(_filters_kernel, seq_len=seq_len),
        grid=(2 * seq_len // FILTER_TILE,),
        in_specs=[full(a) for a in args],
        out_specs=pl.BlockSpec((D_HYENA, FILTER_TILE), lambda i: (0, i)),
        out_shape=jax.ShapeDtypeStruct((D_HYENA, 2 * seq_len), F32),
        compiler_params=_params("arbitrary"),
        name="filters",
    )(*args)


def _hy_proj_kernel(x_ref, g_ref, w_ref, cw_ref, u_ref, x0_ref, p_ref, l_ref, s_ref):
    j = pl.program_id(1)
    nj = pl.num_programs(1) - 1
    half = LANES
    xn = _rms(x_ref[0], g_ref[...]).astype(BF16)
    p_cur = lax.dot_general(w_ref[...], xn, (((1,), (1,)), ((), ())), preferred_element_type=F32)
    s_cur = pltpu.roll(p_cur[:, :half], half - 1, 1)

    @pl.when(j >= 1)
    def _():
        lane = lax.broadcasted_iota(jnp.int32, (1, half), 1)
        p_lo = p_ref[:, :half]
        p_hi = p_ref[:, half:]
        r_lo = pltpu.roll(p_lo, 1, 1)
        r_hi = pltpu.roll(p_hi, 1, 1)
        s_hi = pltpu.roll(p_hi, half - 1, 1)
        r_left = jnp.where(j > 1, l_ref[...], 0.0)
        s_right = jnp.where(j < nj, s_cur, 0.0)
        first = lane == 0
        last = lane == half - 1
        halves = (
            (jnp.where(first, r_left, r_lo), p_lo, jnp.where(last, s_hi, s_ref[...])),
            (jnp.where(first, r_lo, r_hi), p_hi, jnp.where(last, s_right, s_hi)),
        )
        for h, (pm1, p0, pp1) in enumerate(halves):
            uc = cw_ref[0] * pm1 + cw_ref[1] * p0 + cw_ref[2] * pp1 + cw_ref[3]
            cols = slice(h * half, (h + 1) * half)
            x0_ref[0, 0, :, cols] = uc[:D_HYENA]
            u_ref[0, 0, :, cols] = uc[2 * D_HYENA:] * uc[D_HYENA:2 * D_HYENA]
        l_ref[...] = r_hi

    p_ref[...] = p_cur
    s_ref[...] = s_cur


def _hy_proj(x, norm_g, w_hy_t, cw):
    n_seq, seq_len, _ = x.shape
    tb = TOEPLITZ_BLOCK
    nj = seq_len // tb
    out = jax.ShapeDtypeStruct((nj, n_seq, D_HYENA, tb), F32)
    out_spec = pl.BlockSpec((1, 1, D_HYENA, tb), lambda b, j: (jnp.maximum(j - 1, 0), b, 0, 0))
    return pl.pallas_call(
        _hy_proj_kernel,
        grid=(n_seq, nj + 1),
        in_specs=[
            pl.BlockSpec((1, tb, D_MODEL), lambda b, j: (b, jnp.minimum(j, nj - 1), 0)),
            pl.BlockSpec((1, D_MODEL), lambda b, j: (0, 0)),
            pl.BlockSpec((3 * D_HYENA, D_MODEL), lambda b, j: (0, 0)),
            pl.BlockSpec((4, 3 * D_HYENA, LANES), lambda b, j: (0, 0, 0)),
        ],
        out_specs=[out_spec, out_spec],
        out_shape=[out, out],
        scratch_shapes=[pltpu.VMEM((3 * D_HYENA, tb), F32), pltpu.VMEM((3 * D_HYENA, LANES), F32),
                        pltpu.VMEM((3 * D_HYENA, LANES), F32)],
        compiler_params=_params("arbitrary", "arbitrary"),
        name="hy_proj",
    )(x, norm_g, w_hy_t, cw)


def _longconv_kernel(d_ref, k_ref, u_ref, x0_ref, o_ref, s_ref, ub_ref, y_ref):
    c = pl.program_id(0)
    tb = TOEPLITZ_BLOCK
    half = tb // 2
    nj, n_seq, n_ch, _ = u_ref.shape
    rows = nj * BATCH_PAD
    seq_len = nj * tb

    tri = (lax.broadcasted_iota(jnp.int32, (half, half), 1) >= lax.broadcasted_iota(jnp.int32, (half, half), 0))
    s_ref[:, n_seq:, :] = jnp.zeros((nj, BATCH_PAD - n_seq, tb), F32)

    def channel(i, carry):
        def circ(q):
            seg = jnp.broadcast_to(k_ref[i, :, half * q:half * (q + 1)], (half, half))
            return pltpu.roll(seg, 0, 1, stride=1, stride_axis=0)

        s_ref[:, :n_seq, :] = u_ref[:, :, pl.ds(i, 1), :][:, :, 0, :]
        u = s_ref[...].reshape(rows, tb)
        ub_ref[...] = u.astype(BF16)
        y_ref[...] = u * d_ref[c * n_ch + i]

        q = (seq_len - tb * (nj - 1)) // half
        c_prev = circ(q - 1)
        g_prev = jnp.where(tri, c_prev, circ(q - 2)).astype(BF16)
        for d in range(-(nj - 1), nj):
            c_a = circ(q)
            c_b = circ(q + 1)
            g_a = jnp.where(tri, c_a, c_prev).astype(BF16)
            g_b = jnp.where(tri, c_b, c_a).astype(BF16)
            tile = jnp.concatenate(
                [jnp.concatenate([g_a, g_b], axis=1), jnp.concatenate([g_prev, g_a], axis=1)], axis=0)
            c_prev, g_prev = c_b, g_b
            q += 2
            n = BATCH_PAD * (nj - abs(d))
            src = slice(0, n) if d >= 0 else slice(rows - n, rows)
            dst = slice(rows - n, rows) if d >= 0 else slice(0, n)
            y_ref[dst, :] += jnp.dot(ub_ref[src, :], tile, preferred_element_type=F32)

        y = y_ref[...].reshape(nj, BATCH_PAD, tb)[:, :n_seq, :] * x0_ref[:, :, pl.ds(i, 1), :][:, :, 0, :]
        o_ref[:, :, pl.ds(i, 1), :] = y[:, :, None, :]
        return carry

    lax.fori_loop(0, n_ch, channel, 0)


def _longconv(kfull, u, x0, d_skip):
    nj, n_seq, n_ch, tb = u.shape
    seq_len = nj * tb
    blk = pl.BlockSpec((nj, n_seq, SUBLANES, tb), lambda c: (0, 0, c, 0))
    return pl.pallas_call(
        _longconv_kernel,
        grid=(n_ch // SUBLANES,),
        in_specs=[
            pl.BlockSpec(memory_space=pltpu.SMEM),
            pl.BlockSpec((SUBLANES, 1, 2 * seq_len), lambda c: (c, 0, 0)),
            blk, blk,
        ],
        out_specs=blk,
        out_shape=jax.ShapeDtypeStruct(u.shape, F32),
        scratch_shapes=[pltpu.VMEM((nj, BATCH_PAD, tb), F32), pltpu.VMEM((nj * BATCH_PAD, tb), BF16),
                        pltpu.VMEM((nj * BATCH_PAD, tb), F32)],
        compiler_params=_params("arbitrary"),
        name="longconv",
    )(d_skip, kfull.reshape(n_ch, 1, 2 * seq_len), u, x0)


def _in_proj_kernel(x_ref, g_ref, w_ref, cf_ref, rg_ref, gate_ref):
    xn = _rms(x_ref[...], g_ref[...]).astype(BF16)
    p = jnp.dot(xn, w_ref[...], preferred_element_type=F32)
    cf_ref[...] = p[:, :2 * D_CONF]
    rg_ref[...] = p[:, 2 * D_CONF:2 * D_CONF + D_RG]
    gate_ref[...] = p[:, 2 * D_CONF + D_RG:]


def _in_proj(x2d, norm_g, w_rest):
    n_tok = x2d.shape[0]
    n_col = w_rest.shape[1]
    row = lambda w: pl.BlockSpec((PROJ_ROWS, w), lambda i: (i, 0))
    return pl.pallas_call(
        _in_proj_kernel,
        grid=(n_tok // PROJ_ROWS,),
        in_specs=[row(D_MODEL), pl.BlockSpec((1, D_MODEL), lambda i: (0, 0)),
                  pl.BlockSpec((D_MODEL, n_col), lambda i: (0, 0))],
        out_specs=[row(2 * D_CONF), row(D_RG), row(D_MIX)],
        out_shape=[jax.ShapeDtypeStruct((n_tok, w), F32) for w in (2 * D_CONF, D_RG, D_MIX)],
        compiler_params=_params("arbitrary"),
        name="in_proj",
    )(x2d, norm_g, w_rest)


def _glu(p):
    return p[:, :D_CONF] * jax.nn.sigmoid(p[:, D_CONF:])


def _conformer_kernel(p_ref, pp_ref, pn_ref, dw_ref, vec_ref, pw_ref, o_ref, ext_ref):
    t = pl.program_id(1)
    rows = p_ref.shape[1]
    half = (CONF_K - 1) // 2
    n_ext = rows + 2 * CONF_HALO
    ext_ref[0, 0:CONF_HALO, :] = jnp.where(t > 0, _glu(pp_ref[0]), 0.0)
    ext_ref[0, CONF_HALO:CONF_HALO + rows, :] = _glu(p_ref[0])
    ext_ref[0, CONF_HALO + rows:, :] = jnp.where(t < pl.num_programs(1) - 1, _glu(pn_ref[0]), 0.0)
    for r in range(1, SUBLANES):
        ext_ref[r, 0:n_ext - SUBLANES, :] = ext_ref[0, r:r + n_ext - SUBLANES, :]
    vec = vec_ref[...]
    dw_b, ln_g, ln_b, pw_b = vec[0:1], vec[1:2], vec[2:3], vec[3:4]
    dw = dw_ref[...]
    for s in range(rows // CONF_SUB):
        acc = jnp.broadcast_to(dw_b, (CONF_SUB, D_CONF))
        for k in range(CONF_K):
            off = CONF_HALO - half + k
            start = s * CONF_SUB + off - off % SUBLANES
            acc = acc + dw[k:k + 1] * ext_ref[off % SUBLANES, start:start + CONF_SUB, :]
        mu = jnp.mean(acc, axis=-1, keepdims=True)
        xc = acc - mu
        var = jnp.mean(xc * xc, axis=-1, keepdims=True)
        y = xc * lax.rsqrt(var + EPS) * ln_g + ln_b
        y = y * jax.nn.sigmoid(y)
        o_ref[0, s * CONF_SUB:(s + 1) * CONF_SUB, :] = (
            jnp.dot(y.astype(BF16), pw_ref[...], preferred_element_type=F32) + pw_b)


def _conformer(p_cf, dw_w, vec, pw_w):
    n_seq, seq_len, _ = p_cf.shape
    per = CONF_ROWS // CONF_HALO
    last = seq_len // CONF_HALO - 1
    return pl.pallas_call(
        _conformer_kernel,
        grid=(n_seq, seq_len // CONF_ROWS),
        in_specs=[
            pl.BlockSpec((1, CONF_ROWS, 2 * D_CONF), lambda b, t: (b, t, 0)),
            pl.BlockSpec((1, CONF_HALO, 2 * D_CONF), lambda b, t: (b, jnp.maximum(t * per - 1, 0), 0)),
            pl.BlockSpec((1, CONF_HALO, 2 * D_CONF), lambda b, t: (b, jnp.minimum((t + 1) * per, last), 0)),
            pl.BlockSpec((CONF_K, D_CONF), lambda b, t: (0, 0)),
            pl.BlockSpec((SUBLANES, D_CONF), lambda b, t: (0, 0)),
            pl.BlockSpec((D_CONF, D_CONF), lambda b, t: (0, 0)),
        ],
        out_specs=pl.BlockSpec((1, CONF_ROWS, D_CONF), lambda b, t: (b, t, 0)),
        out_shape=jax.ShapeDtypeStruct((n_seq, seq_len, D_CONF), F32),
        scratch_shapes=[pltpu.VMEM((SUBLANES, CONF_ROWS + 2 * CONF_HALO, D_CONF), F32)],
        compiler_params=_params("arbitrary", "arbitrary"),
        name="conformer",
    )(p_cf, p_cf, p_cf, dw_w, vec, pw_w)


def _shift_sub(x, s, fill, reverse):
    r = x.shape[1]
    sub = lax.broadcasted_iota(jnp.int32, (1, r, 1), 1)
    if reverse:
        return jnp.where(sub < r - s, pltpu.roll(x, r - s, 1), fill)
    return jnp.where(sub >= s, pltpu.roll(x, s, 1), fill)


def _shift_tile(x, k, fill, reverse):
    pad = jnp.full((k,) + x.shape[1:], fill, x.dtype)
    if reverse:
        return jnp.concatenate([x[k:], pad], axis=0)
    return jnp.concatenate([pad, x[:x.shape[0] - k]], axis=0)


def _scan_chunk(a, b, carry, reverse):
    rows, lanes = a.shape
    n = rows // SUBLANES
    a = a.reshape(n, SUBLANES, lanes)
    b = b.reshape(n, SUBLANES, lanes)
    s = 1
    while s < SUBLANES:
        b = b + a * _shift_sub(b, s, 0.0, reverse)
        a = a * _shift_sub(a, s, 1.0, reverse)
        s *= 2
    edge = 0 if reverse else SUBLANES - 1
    at = jnp.broadcast_to(a[:, edge:edge + 1, :], a.shape)
    bt = jnp.broadcast_to(b[:, edge:edge + 1, :], b.shape)
    k = 1
    while k < n:
        bt = bt + at * _shift_tile(bt, k, 0.0, reverse)
        at = at * _shift_tile(at, k, 1.0, reverse)
        k *= 2
    h_in = _shift_tile(bt, 1, 0.0, reverse) + _shift_tile(at, 1, 1.0, reverse) * carry
    return (b + a * h_in).reshape(rows, lanes)


def _rglru_kernel(p_ref, cw_ref, w_ref, vec_ref, o_ref, xp_ref, xr_ref):
    seq_len = p_ref.shape[1]
    lanes = p_ref.shape[2]
    n_chunk = seq_len // RG_CHUNK
    zeros = jnp.zeros((RG_PAD, lanes), F32)
    xp_ref[0:RG_PAD, :] = zeros
    xp_ref[RG_PAD + seq_len:, :] = zeros
    xp_ref[RG_PAD:RG_PAD + seq_len, :] = p_ref[0]
    cw = cw_ref[...]
    vec = vec_ref[0]
    conv_b = vec[0:1]

    def gates(xr, d):
        g = jnp.dot(xr.astype(BF16), w_ref[0, d], preferred_element_type=F32)
        r = jax.nn.sigmoid(g[:, :lanes] + vec[1 + 4 * d:2 + 4 * d])
        i = jax.nn.sigmoid(g[:, lanes:] + vec[2 + 4 * d:3 + 4 * d])
        sp = jax.nn.softplus(-vec[3 + 4 * d:4 + 4 * d])
        a = jnp.exp(-RG_C * r * sp)
        return a, jnp.sqrt(1.0 - a * a) * (i * xr)

    def fwd(c, carry):
        t0 = pl.multiple_of(c * RG_CHUNK, RG_CHUNK)
        x = xp_ref[pl.ds(t0, RG_CHUNK + 2 * RG_PAD), :]
        xr = conv_b
        for k in range(4):
            xr = xr + cw[k:k + 1] * x[RG_PAD - 2 + k:RG_PAD - 2 + k + RG_CHUNK]
        xr_ref[pl.ds(t0, RG_CHUNK), :] = xr
        a, bt = gates(xr, 0)
        h = _scan_chunk(a, bt, carry, False)
        o_ref[0, pl.ds(t0, RG_CHUNK), :] = h
        return h[RG_CHUNK - 1:RG_CHUNK]

    lax.fori_loop(0, n_chunk, fwd, jnp.zeros((1, lanes), F32))

    def bwd(i, carry):
        t0 = pl.multiple_of((n_chunk - 1 - i) * RG_CHUNK, RG_CHUNK)
        a, bt = gates(xr_ref[pl.ds(t0, RG_CHUNK), :], 1)
        h = _scan_chunk(a, bt, carry, True)
        o_ref[0, pl.ds(t0, RG_CHUNK), :] += h
        return h[0:1]

    lax.fori_loop(0, n_chunk, bwd, jnp.zeros((1, lanes), F32))


def _rglru(p_rg, conv_w, w_cat, vec):
    n_seq, seq_len, _ = p_rg.shape
    n_cb = D_RG // LANES
    blk = pl.BlockSpec((1, seq_len, LANES), lambda b, c: (b, 0, c))
    return pl.pallas_call(
        _rglru_kernel,
        grid=(n_seq, n_cb),
        in_specs=[
            blk,
            pl.BlockSpec((4, LANES), lambda b, c: (0, c)),
            pl.BlockSpec((1, 2, LANES, 2 * LANES), lambda b, c: (c, 0, 0, 0)),
            pl.BlockSpec((1, 2 * SUBLANES, LANES), lambda b, c: (c, 0, 0)),
        ],
        out_specs=blk,
        out_shape=jax.ShapeDtypeStruct((n_seq, seq_len, D_RG), F32),
        scratch_shapes=[pltpu.VMEM((seq_len + 2 * RG_PAD, LANES), F32), pltpu.VMEM((seq_len, LANES), F32)],
        compiler_params=_params("arbitrary", "arbitrary"),
        name="rglru",
    )(p_rg, conv_w, w_cat, vec)


def _out_proj_kernel(yh_ref, ycf_ref, yrg_ref, gate_ref, x_ref, gg_ref, w_ref, fg_ref, *o_refs, n_first):
    b = pl.program_id(1)
    gg = gg_ref[...]
    y = jnp.concatenate([
        _rms(yh_ref[0, 0].T, gg[:, :D_HYENA]),
        _rms(ycf_ref[0], gg[:, D_HYENA:D_HYENA + D_CONF]),
        _rms(yrg_ref[0], gg[:, D_HYENA + D_CONF:]),
    ], axis=-1)
    gate = gate_ref[0]
    y = y * (gate * jax.nn.sigmoid(gate))
    out = x_ref[0] + jnp.dot(y.astype(BF16), w_ref[...], preferred_element_type=F32)
    if len(o_refs) == 1:
        o_refs[0][0] = out
        return
    out = _rms(out, fg_ref[...])

    @pl.when(b < n_first)
    def _():
        o_refs[0][0] = out

    @pl.when(b >= n_first)
    def _():
        o_refs[1][0] = out


def _out_proj(y_hy, y_cf, y_rg, gate, x, grp_g, w_out, final_g, n_first):
    n_seq, seq_len, _ = x.shape
    nj, _, n_ch, tb = y_hy.shape
    tok = lambda w: pl.BlockSpec((1, tb, w), lambda j, b: (b, j, 0))
    vec = lambda w: pl.BlockSpec((1, w), lambda j, b: (0, 0))
    if n_first is None:
        out_specs = [tok(D_MODEL)]
        out_shape = [jax.ShapeDtypeStruct((n_seq, seq_len, D_MODEL), F32)]
    else:
        out_specs = [
            pl.BlockSpec((1, tb, D_MODEL), lambda j, b: (jnp.minimum(b, n_first - 1), j, 0)),
            pl.BlockSpec((1, tb, D_MODEL), lambda j, b: (jnp.maximum(b - n_first, 0), j, 0)),
        ]
        out_shape = [jax.ShapeDtypeStruct((n, seq_len, D_MODEL), F32) for n in (n_first, n_seq - n_first)]
    return pl.pallas_call(
        functools.partial(_out_proj_kernel, n_first=n_first),
        grid=(nj, n_seq),
        in_specs=[
            pl.BlockSpec((1, 1, n_ch, tb), lambda j, b: (j, b, 0, 0)),
            tok(D_CONF), tok(D_RG), tok(D_MIX), tok(D_MODEL),
            vec(D_MIX), pl.BlockSpec((D_MIX, D_MODEL), lambda j, b: (0, 0)), vec(D_MODEL),
        ],
        out_specs=out_specs,
        out_shape=out_shape,
        compiler_params=_params("arbitrary", "arbitrary"),
        name="out_proj",
    )(y_hy, y_cf, y_rg, gate, x, grp_g, w_out, final_g)


def _rg_gate_weights(wa, wx):
    per = LANES // GROUP_WIDTH
    n_cb = D_RG // LANES

    def diag(w):
        w = w.reshape(n_cb, per, GROUP_WIDTH, GROUP_WIDTH)
        eye = jnp.eye(per, dtype=w.dtype)
        return jnp.einsum("cpij,pq->cpiqj", w, eye).reshape(n_cb, LANES, LANES)

    dirs = [jnp.concatenate([diag(wa[d]), diag(wx[d])], axis=-1) for d in range(2)]
    return jnp.stack(dirs, axis=1).astype(BF16)


def _layer(x, n_first, final_g, norm_g, w_in, hy_conv_w, hy_conv_b, hy_w1, hy_b1, hy_w2, hy_b2, hy_w3, hy_b3,
           hy_w4, hy_freq, hy_d, cf_dw_w, cf_dw_b, cf_ln_g, cf_ln_b, cf_pw_w, cf_pw_b, rg_conv_w, rg_conv_b,
           rg_wa, rg_ba, rg_wx, rg_bx, rg_lam, grp_g, w_out):
    n_seq, seq_len, _ = x.shape
    s1 = 3 * D_HYENA
    g_row = norm_g.reshape(1, D_MODEL)

    kfull = _filters(hy_w1, hy_b1, hy_w2, hy_b2, hy_w3, hy_b3, hy_w4, hy_freq, seq_len)
    cw = jnp.concatenate([hy_conv_w, hy_conv_b[None]], axis=0)
    cw = jnp.broadcast_to(cw[:, :, None], (4, s1, LANES))
    u, x0 = _hy_proj(x, g_row, w_in[:, :s1].T.astype(BF16), cw)
    y_hy = _longconv(kfull, u, x0, hy_d)

    p_cf, p_rg, gate = _in_proj(x.reshape(n_seq * seq_len, D_MODEL), g_row, w_in[:, s1:].astype(BF16))
    zrow = jnp.zeros((D_CONF,), F32)
    cf_vec = jnp.stack([cf_dw_b, cf_ln_g, cf_ln_b, cf_pw_b, zrow, zrow, zrow, zrow])
    y_cf = _conformer(p_cf.reshape(n_seq, seq_len, 2 * D_CONF), cf_dw_w, cf_vec, cf_pw_w.astype(BF16))

    n_cb = D_RG // LANES
    zr = jnp.zeros((D_RG,), F32)
    rg_rows = [rg_conv_b, rg_ba[0], rg_bx[0], rg_lam[0], zr, rg_ba[1], rg_bx[1], rg_lam[1]] + [zr] * 8
    rg_vec = jnp.stack(rg_rows).reshape(2 * SUBLANES, n_cb, LANES).transpose(1, 0, 2)
    y_rg = _rglru(p_rg.reshape(n_seq, seq_len, D_RG), rg_conv_w, _rg_gate_weights(rg_wa, rg_wx), rg_vec)

    return _out_proj(y_hy, y_cf, y_rg, gate.reshape(n_seq, seq_len, D_MIX), x, grp_g.reshape(1, D_MIX),
                     w_out.astype(BF16), final_g.reshape(1, D_MODEL), n_first)


def kernel(x_prompt, x_sample, norm_g, w_in, hy_conv_w, hy_conv_b, hy_w1, hy_b1, hy_w2, hy_b2, hy_w3, hy_b3, hy_w4, hy_freq, hy_d, cf_dw_w, cf_dw_b, cf_ln_g, cf_ln_b, cf_pw_w, cf_pw_b, rg_conv_w, rg_conv_b, rg_wa, rg_ba, rg_wx, rg_bx, rg_lam, grp_g, w_out, final_g):
    layer_params = (norm_g, w_in, hy_conv_w, hy_conv_b, hy_w1, hy_b1, hy_w2, hy_b2, hy_w3, hy_b3, hy_w4, hy_freq,
                    hy_d, cf_dw_w, cf_dw_b, cf_ln_g, cf_ln_b, cf_pw_w, cf_pw_b, rg_conv_w, rg_conv_b, rg_wa,
                    rg_ba, rg_wx, rg_bx, rg_lam, grp_g, w_out)
    assert x_prompt.shape[1:] == x_sample.shape[1:]
    n_prompt = x_prompt.shape[0]
    assert n_prompt + x_sample.shape[0] <= BATCH_PAD
    depth = norm_g.shape[0]
    x = jnp.concatenate([x_prompt, x_sample], axis=0)
    for l in range(depth - 1):
        (x,) = _layer(x, None, final_g, *[w[l] for w in layer_params])
    y_prompt, y_sample = _layer(x, n_prompt, final_g, *[w[depth - 1] for w in layer_params])
    return (y_prompt, y_sample)
```

```python
import functools
import math

import jax
import jax.numpy as jnp
from jax import lax
from jax.experimental import pallas as pl
from jax.experimental.pallas import tpu as pltpu

F32 = jnp.float32
BF16 = jnp.bfloat16

D_MODEL = 1024
D_HYENA = 384
D_CONF = 256
D_RG = 384
D_MIX = D_HYENA + D_CONF + D_RG
GROUP_WIDTH = 64
HYENA_BANDS = 16
HYENA_ORDER = 64
HYENA_TARGET = 1e-2
HYENA_FAST_DECAY_PCT = 0.3
HYENA_SLOW_DECAY_PCT = 1.5
CONF_K = 31
RG_C = 8.0
EPS = 1e-6

LANES = 128
SUBLANES = 8
MXU_DIM = 256
VMEM_LIMIT_BYTES = 56 * 1024 * 1024

TOEPLITZ_BLOCK = MXU_DIM
BATCH_PAD = 16
FILTER_TILE = 2048
PROJ_ROWS = 512
CONF_ROWS = 512
CONF_HALO = 16
CONF_SUB = 64
RG_CHUNK = 256
RG_PAD = 8


def _params(*sem):
    return pltpu.CompilerParams(dimension_semantics=sem, vmem_limit_bytes=VMEM_LIMIT_BYTES)


def _rms(x, g):
    return x * lax.rsqrt(jnp.mean(x * x, axis=-1, keepdims=True) + EPS) * g


def _filters_kernel(w1t_ref, w1c_ref, w1s_ref, b1_ref, w2_ref, b2_ref, w3_ref, b3_ref, w4_ref,
                    fr_ref, o_ref, *, seq_len):
    i = pl.program_id(0)
    tile = o_ref.shape[1]
    m = i * tile + lax.broadcasted_iota(jnp.int32, (1, tile), 1)
    pos = jnp.abs(m - seq_len).astype(F32)
    t = pos / (seq_len - 1.0)
    w = (2.0 * math.pi / seq_len) * pos
    band = lax.broadcasted_iota(jnp.int32, (HYENA_BANDS, 1), 0).astype(F32)
    f = 1e-4 + band * ((HYENA_BANDS - 1 - 1e-4) / (HYENA_BANDS - 1))
    fw = f * w
    hi = lax.Precision.HIGHEST
    fr = fr_ref[...]
    pre = (w1t_ref[...] * t + jnp.dot(w1c_ref[...], jnp.cos(fw), precision=hi)
           - jnp.dot(w1s_ref[...], jnp.sin(fw), precision=hi))
    h = jnp.sin(fr * (pre + b1_ref[...]))
    h = jnp.sin(fr * (jnp.dot(w2_ref[...], h, precision=hi) + b2_ref[...]))
    h = jnp.sin(fr * (jnp.dot(w3_ref[...], h, precision=hi) + b3_ref[...]))
    k = jnp.dot(w4_ref[...], h, precision=hi)
    max_decay = math.log(HYENA_TARGET) / HYENA_FAST_DECAY_PCT
    min_decay = math.log(HYENA_TARGET) / HYENA_SLOW_DECAY_PCT
    ch = lax.broadcasted_iota(jnp.int32, (D_HYENA, 1), 0).astype(F32)
    delta = jnp.abs(min_decay + ch * ((max_decay - min_decay) / (D_HYENA - 1)))
    decay = jnp.exp(-t * delta)
    k = jnp.where(m >= seq_len, k[:D_HYENA], k[D_HYENA:]) * decay
    o_ref[...] = jnp.where(m == 0, 0.0, k)


def _filters(w1, b1, w2, b2, w3, b3, w4, freq, seq_len):
    w1t = w1.T
    col = lambda v: v.reshape(-1, 1)
    args = (w1t[:, 0:1], w1t[:, 1:1 + HYENA_BANDS], w1t[:, 1 + HYENA_BANDS:], col(b1), w2.T, col(b2),
            w3.T, col(b3), w4.T, col(freq))
    full = lambda a: pl.BlockSpec(a.shape, lambda i: (0, 0))
    return pl.pallas_call(
        functools.partial(_filters_kernel, seq_len=seq_len),
        grid=(2 * seq_len // FILTER_TILE,),
        in_specs=[full(a) for a in args],
        out_specs=pl.BlockSpec((D_HYENA, FILTER_TILE), lambda i: (0, i)),
        out_shape=jax.ShapeDtypeStruct((D_HYENA, 2 * seq_len), F32),
        compiler_params=_params("arbitrary"),
        name="filters",
    )(*args)


def _hy_proj_kernel(x_ref, g_ref, w_ref, cw_ref, u_ref, x0_ref, p_ref, l_ref, s_ref):
    j = pl.program_id(1)
    nj = pl.num_programs(1) - 1
    half = LANES

    @pl.when(j == 0)
    def _():
        p_ref[...] = jnp.zeros(p_ref.shape, F32)
        l_ref[...] = jnp.zeros(l_ref.shape, F32)
        s_ref[...] = jnp.zeros(s_ref.shape, F32)

    xn = _rms(x_ref[0], g_ref[...]).astype(BF16)
    p_cur = lax.dot_general(w_ref[...], xn, (((1,), (1,)), ((), ())), preferred_element_type=F32)
    s_cur = pltpu.roll(p_cur[:, :half], half - 1, 1)

    lane = lax.broadcasted_iota(jnp.int32, (1, half), 1)
    p_lo = p_ref[:, :half]
    p_hi = p_ref[:, half:]
    r_lo = pltpu.roll(p_lo, 1, 1)
    r_hi = pltpu.roll(p_hi, 1, 1)
    s_hi = pltpu.roll(p_hi, half - 1, 1)
    s_right = jnp.where(j < nj, s_cur, 0.0)
    first = lane == 0
    last = lane == half - 1
    halves = (
        (jnp.where(first, l_ref[...], r_lo), p_lo, jnp.where(last, s_hi, s_ref[...])),
        (jnp.where(first, r_lo, r_hi), p_hi, jnp.where(last, s_right, s_hi)),
    )
    for h, (pm1, p0, pp1) in enumerate(halves):
        uc = cw_ref[0] * pm1 + cw_ref[1] * p0 + cw_ref[2] * pp1 + cw_ref[3]
        cols = slice(h * half, (h + 1) * half)
        x0_ref[0, 0, :, cols] = uc[:D_HYENA]
        u_ref[0, 0, :, cols] = uc[2 * D_HYENA:] * uc[D_HYENA:2 * D_HYENA]
    l_ref[...] = r_hi
    p_ref[...] = p_cur
    s_ref[...] = s_cur


def _hy_proj(x, norm_g, w_hy_t, cw):
    n_seq, seq_len, _ = x.shape
    tb = TOEPLITZ_BLOCK
    nj = seq_len // tb
    out = jax.ShapeDtypeStruct((nj, n_seq, D_HYENA, tb), F32)
    out_spec = pl.BlockSpec((1, 1, D_HYENA, tb), lambda b, j: (jnp.maximum(j - 1, 0), b, 0, 0))
    return pl.pallas_call(
        _hy_proj_kernel,
        grid=(n_seq, nj + 1),
        in_specs=[
            pl.BlockSpec((1, tb, D_MODEL), lambda b, j: (b, jnp.minimum(j, nj - 1), 0)),
            pl.BlockSpec((1, D_MODEL), lambda b, j: (0, 0)),
            pl.BlockSpec((3 * D_HYENA, D_MODEL), lambda b, j: (0, 0)),
            pl.BlockSpec((4, 3 * D_HYENA, LANES), lambda b, j: (0, 0, 0)),
        ],
        out_specs=[out_spec, out_spec],
        out_shape=[out, out],
        scratch_shapes=[pltpu.VMEM((3 * D_HYENA, tb), F32), pltpu.VMEM((3 * D_HYENA, LANES), F32),
                        pltpu.VMEM((3 * D_HYENA, LANES), F32)],
        compiler_params=_params("arbitrary", "arbitrary"),
        name="hy_proj",
    )(x, norm_g, w_hy_t, cw)


def _longconv_kernel(d_ref, k_ref, u_ref, x0_ref, o_ref, s_ref, ub_ref, y_ref):
    c = pl.program_id(0)
    tb = TOEPLITZ_BLOCK
    half = tb // 2
    nj, n_seq, n_ch, _ = u_ref.shape
    rows = nj * BATCH_PAD
    seq_len = nj * tb

    tri = (lax.broadcasted_iota(jnp.int32, (half, half), 1) >= lax.broadcasted_iota(jnp.int32, (half, half), 0))
    s_ref[:, n_seq:, :] = jnp.zeros((nj, BATCH_PAD - n_seq, tb), F32)

    def channel(i, carry):
        def circ(q):
            seg = jnp.broadcast_to(k_ref[i, :, half * q:half * (q + 1)], (half, half))
            return pltpu.roll(seg, 0, 1, stride=1, stride_axis=0)

        s_ref[:, :n_seq, :] = u_ref[:, :, pl.ds(i, 1), :][:, :, 0, :]
        u = s_ref[...].reshape(rows, tb)
        ub_ref[...] = u.astype(BF16)
        y_ref[...] = u * d_ref[c * n_ch + i]

        q = (seq_len - tb * (nj - 1)) // half
        c_prev = circ(q - 1)
        g_prev = jnp.where(tri, c_prev, circ(q - 2)).astype(BF16)
        for d in range(-(nj - 1), nj):
            c_a = circ(q)
            c_b = circ(q + 1)
            g_a = jnp.where(tri, c_a, c_prev).astype(BF16)
            g_b = jnp.where(tri, c_b, c_a).astype(BF16)
            tile = jnp.concatenate(
                [jnp.concatenate([g_a, g_b], axis=1), jnp.concatenate([g_prev, g_a], axis=1)], axis=0)
            c_prev, g_prev = c_b, g_b
            q += 2
            n = BATCH_PAD * (nj - abs(d))
            src = slice(0, n) if d >= 0 else slice(rows - n, rows)
            dst = slice(rows - n, rows) if d >= 0 else slice(0, n)
            y_ref[dst, :] += jnp.dot(ub_ref[src, :], tile, preferred_element_type=F32)

        y = y_ref[...].reshape(nj, BATCH_PAD, tb)[:, :n_seq, :] * x0_ref[:, :, pl.ds(i, 1), :][:, :, 0, :]
        o_ref[:, :, pl.ds(i, 1), :] = y[:, :, None, :]
        return carry

    lax.fori_loop(0, n_ch, channel, 0)


def _longconv(kfull, u, x0, d_skip):
    nj, n_seq, n_ch, tb = u.shape
    seq_len = nj * tb
    blk = pl.BlockSpec((nj, n_seq, SUBLANES, tb), lambda c: (0, 0, c, 0))
    return pl.pallas_call(
        _longconv_kernel,
        grid=(n_ch // SUBLANES,),
        in_specs=[
            pl.BlockSpec(memory_space=pltpu.SMEM),
            pl.BlockSpec((SUBLANES, 1, 2 * seq_len), lambda c: (c, 0, 0)),
            blk, blk,
        ],
        out_specs=blk,
        out_shape=jax.ShapeDtypeStruct(u.shape, F32),
        scratch_shapes=[pltpu.VMEM((nj, BATCH_PAD, tb), F32), pltpu.VMEM((nj * BATCH_PAD, tb), BF16),
                        pltpu.VMEM((nj * BATCH_PAD, tb), F32)],
        compiler_params=_params("arbitrary"),
        name="longconv",
    )(d_skip, kfull.reshape(n_ch, 1, 2 * seq_len), u, x0)


def _in_proj_kernel(x_ref, g_ref, w_ref, cf_ref, rg_ref, gate_ref):
    xn = _rms(x_ref[...], g_ref[...]).astype(BF16)
    p = jnp.dot(xn, w_ref[...], preferred_element_type=F32).astype(BF16)
    cf_ref[...] = p[:, :2 * D_CONF]
    rg_ref[...] = p[:, 2 * D_CONF:2 * D_CONF + D_RG]
    gate_ref[...] = p[:, 2 * D_CONF + D_RG:]


def _in_proj(x2d, norm_g, w_rest):
    n_tok = x2d.shape[0]
    n_col = w_rest.shape[1]
    row = lambda w: pl.BlockSpec((PROJ_ROWS, w), lambda i: (i, 0))
    return pl.pallas_call(
        _in_proj_kernel,
        grid=(n_tok // PROJ_ROWS,),
        in_specs=[row(D_MODEL), pl.BlockSpec((1, D_MODEL), lambda i: (0, 0)),
                  pl.BlockSpec((D_MODEL, n_col), lambda i: (0, 0))],
        out_specs=[row(2 * D_CONF), row(D_RG), row(D_MIX)],
        out_shape=[jax.ShapeDtypeStruct((n_tok, w), BF16) for w in (2 * D_CONF, D_RG, D_MIX)],
        compiler_params=_params("arbitrary"),
        name="in_proj",
    )(x2d, norm_g, w_rest)


def _glu(p):
    p = p.astype(F32)
    return p[:, :D_CONF] * jax.nn.sigmoid(p[:, D_CONF:])


def _conformer_kernel(p_ref, pp_ref, pn_ref, dw_ref, vec_ref, pw_ref, o_ref, ext_ref):
    t = pl.program_id(1)
    rows = p_ref.shape[1]
    half = (CONF_K - 1) // 2
    n_ext = rows + 2 * CONF_HALO
    ext_ref[0, 0:CONF_HALO, :] = jnp.where(t > 0, _glu(pp_ref[0]), 0.0)
    ext_ref[0, CONF_HALO:CONF_HALO + rows, :] = _glu(p_ref[0])
    ext_ref[0, CONF_HALO + rows:, :] = jnp.where(t < pl.num_programs(1) - 1, _glu(pn_ref[0]), 0.0)
    for r in range(1, SUBLANES):
        ext_ref[r, 0:n_ext - SUBLANES, :] = ext_ref[0, r:r + n_ext - SUBLANES, :]
    vec = vec_ref[...]
    dw_b, ln_g, ln_b, pw_b = vec[0:1], vec[1:2], vec[2:3], vec[3:4]
    dw = dw_ref[...]
    for s in range(rows // CONF_SUB):
        acc = jnp.broadcast_to(dw_b, (CONF_SUB, D_CONF))
        for k in range(CONF_K):
            off = CONF_HALO - half + k
            start = s * CONF_SUB + off - off % SUBLANES
            acc = acc + dw[k:k + 1] * ext_ref[off % SUBLANES, start:start + CONF_SUB, :]
        mu = jnp.mean(acc, axis=-1, keepdims=True)
        xc = acc - mu
        var = jnp.mean(xc * xc, axis=-1, keepdims=True)
        y = xc * lax.rsqrt(var + EPS) * ln_g + ln_b
        y = y * jax.nn.sigmoid(y)
        o_ref[0, s * CONF_SUB:(s + 1) * CONF_SUB, :] = (
            jnp.dot(y.astype(BF16), pw_ref[...], preferred_element_type=F32) + pw_b).astype(o_ref.dtype)


def _conformer(p_cf, dw_w, vec, pw_w):
    n_seq, seq_len, _ = p_cf.shape
    per = CONF_ROWS // CONF_HALO
    last = seq_len // CONF_HALO - 1
    return pl.pallas_call(
        _conformer_kernel,
        grid=(n_seq, seq_len // CONF_ROWS),
        in_specs=[
            pl.BlockSpec((1, CONF_ROWS, 2 * D_CONF), lambda b, t: (b, t, 0)),
            pl.BlockSpec((1, CONF_HALO, 2 * D_CONF), lambda b, t: (b, jnp.maximum(t * per - 1, 0), 0)),
            pl.BlockSpec((1, CONF_HALO, 2 * D_CONF), lambda b, t: (b, jnp.minimum((t + 1) * per, last), 0)),
            pl.BlockSpec((CONF_K, D_CONF), lambda b, t: (0, 0)),
            pl.BlockSpec((SUBLANES, D_CONF), lambda b, t: (0, 0)),
            pl.BlockSpec((D_CONF, D_CONF), lambda b, t: (0, 0)),
        ],
        out_specs=pl.BlockSpec((1, CONF_ROWS, D_CONF), lambda b, t: (b, t, 0)),
        out_shape=jax.ShapeDtypeStruct((n_seq, seq_len, D_CONF), BF16),
        scratch_shapes=[pltpu.VMEM((SUBLANES, CONF_ROWS + 2 * CONF_HALO, D_CONF), F32)],
        compiler_params=_params("arbitrary", "arbitrary"),
        name="conformer",
    )(p_cf, p_cf, p_cf, dw_w, vec, pw_w)


def _shift_sub(x, s, fill, reverse):
    r = x.shape[1]
    sub = lax.broadcasted_iota(jnp.int32, (1, r, 1), 1)
    if reverse:
        return jnp.where(sub < r - s, pltpu.roll(x, r - s, 1), fill)
    return jnp.where(sub >= s, pltpu.roll(x, s, 1), fill)


def _scan_chunk(a, b, carry, reverse):
    rows, lanes = a.shape
    n = rows // SUBLANES
    a = a.reshape(n, SUBLANES, lanes)
    b = b.reshape(n, SUBLANES, lanes)
    s = 1
    while s < SUBLANES:
        b = b + a * _shift_sub(b, s, 0.0, reverse)
        a = a * _shift_sub(a, s, 1.0, reverse)
        s *= 2
    edge = 0 if reverse else SUBLANES - 1
    at = jnp.broadcast_to(a[:, edge:edge + 1, :], a.shape)
    bt = jnp.broadcast_to(b[:, edge:edge + 1, :], b.shape)
    h_in = [None] * n
    for v in (range(n - 1, -1, -1) if reverse else range(n)):
        h_in[v] = carry
        carry = bt[v] + at[v] * carry
    h = b + a * jnp.stack(h_in)
    return h.reshape(rows, lanes), carry


def _rglru_kernel(p_ref, cw_ref, w_ref, vec_ref, o_ref, xp_ref, xr_ref, hf_ref):
    seq_len = p_ref.shape[1]
    lanes = p_ref.shape[2]
    n_chunk = seq_len // RG_CHUNK
    zeros = jnp.zeros((RG_PAD, lanes), F32)
    xp_ref[0:RG_PAD, :] = zeros
    xp_ref[RG_PAD + seq_len:, :] = zeros
    xp_ref[RG_PAD:RG_PAD + seq_len, :] = p_ref[0].astype(F32)
    cw = cw_ref[...]
    vec = vec_ref[0]
    conv_b = vec[0:1]

    def gates(xr, d):
        g = jnp.dot(xr.astype(BF16), w_ref[0, d], preferred_element_type=F32)
        r = jax.nn.sigmoid(g[:, :lanes] + vec[1 + 4 * d:2 + 4 * d])
        i = jax.nn.sigmoid(g[:, lanes:] + vec[2 + 4 * d:3 + 4 * d])
        sp = jax.nn.softplus(-vec[3 + 4 * d:4 + 4 * d])
        a = jnp.exp(-RG_C * r * sp)
        return a, jnp.sqrt(1.0 - a * a) * (i * xr)

    def fwd(c, carry):
        t0 = pl.multiple_of(c * RG_CHUNK, RG_CHUNK)
        x = xp_ref[pl.ds(t0, RG_CHUNK + 2 * RG_PAD), :]
        xr = conv_b
        for k in range(4):
            xr = xr + cw[k:k + 1] * x[RG_PAD - 2 + k:RG_PAD - 2 + k + RG_CHUNK]
        xr_ref[pl.ds(t0, RG_CHUNK), :] = xr
        a, bt = gates(xr, 0)
        h, carry = _scan_chunk(a, bt, carry, False)
        hf_ref[pl.ds(t0, RG_CHUNK), :] = h
        return carry

    lax.fori_loop(0, n_chunk, fwd, jnp.zeros((SUBLANES, lanes), F32))

    def bwd(i, carry):
        t0 = pl.multiple_of((n_chunk - 1 - i) * RG_CHUNK, RG_CHUNK)
        a, bt = gates(xr_ref[pl.ds(t0, RG_CHUNK), :], 1)
        h, carry = _scan_chunk(a, bt, carry, True)
        o_ref[0, pl.ds(t0, RG_CHUNK), :] = (hf_ref[pl.ds(t0, RG_CHUNK), :] + h).astype(o_ref.dtype)
        return carry

    lax.fori_loop(0, n_chunk, bwd, jnp.zeros((SUBLANES, lanes), F32))


def _rglru(p_rg, conv_w, w_cat, vec):
    n_seq, seq_len, _ = p_rg.shape
    n_cb = D_RG // LANES
    blk = pl.BlockSpec((1, seq_len, LANES), lambda b, c: (b, 0, c))
    return pl.pallas_call(
        _rglru_kernel,
        grid=(n_seq, n_cb),
        in_specs=[
            blk,
            pl.BlockSpec((4, LANES), lambda b, c: (0, c)),
            pl.BlockSpec((1, 2, LANES, 2 * LANES), lambda b, c: (c, 0, 0, 0)),
            pl.BlockSpec((1, 2 * SUBLANES, LANES), lambda b, c: (c, 0, 0)),
        ],
        out_specs=blk,
        out_shape=jax.ShapeDtypeStruct((n_seq, seq_len, D_RG), BF16),
        scratch_shapes=[pltpu.VMEM((seq_len + 2 * RG_PAD, LANES), F32), pltpu.VMEM((seq_len, LANES), F32),
                        pltpu.VMEM((seq_len, LANES), F32)],
        compiler_params=_params("arbitrary", "arbitrary"),
        name="rglru",
    )(p_rg, conv_w, w_cat, vec)


def _out_proj_kernel(yh_ref, ycf_ref, yrg_ref, gate_ref, x_ref, gg_ref, w_ref, fg_ref, *o_refs, n_first):
    b = pl.program_id(1)
    gg = gg_ref[...]
    y = jnp.concatenate([
        _rms(yh_ref[0, 0].T, gg[:, :D_HYENA]),
        _rms(ycf_ref[0].astype(F32), gg[:, D_HYENA:D_HYENA + D_CONF]),
        _rms(yrg_ref[0].astype(F32), gg[:, D_HYENA + D_CONF:]),
    ], axis=-1)
    gate = gate_ref[0].astype(F32)
    y = y * (gate * jax.nn.sigmoid(gate))
    out = x_ref[0] + jnp.dot(y.astype(BF16), w_ref[...], preferred_element_type=F32)
    if len(o_refs) == 1:
        o_refs[0][0] = out
        return
    out = _rms(out, fg_ref[...])

    @pl.when(b < n_first)
    def _():
        o_refs[0][0] = out

    @pl.when(b >= n_first)
    def _():
        o_refs[1][0] = out


def _out_proj(y_hy, y_cf, y_rg, gate, x, grp_g, w_out, final_g, n_first):
    n_seq, seq_len, _ = x.shape
    nj, _, n_ch, tb = y_hy.shape
    tok = lambda w: pl.BlockSpec((1, tb, w), lambda j, b: (b, j, 0))
    vec = lambda w: pl.BlockSpec((1, w), lambda j, b: (0, 0))
    if n_first is None:
        out_specs = [tok(D_MODEL)]
        out_shape = [jax.ShapeDtypeStruct((n_seq, seq_len, D_MODEL), F32)]
    else:
        out_specs = [
            pl.BlockSpec((1, tb, D_MODEL), lambda j, b: (jnp.minimum(b, n_first - 1), j, 0)),
            pl.BlockSpec((1, tb, D_MODEL), lambda j, b: (jnp.maximum(b - n_first, 0), j, 0)),
        ]
        out_shape = [jax.ShapeDtypeStruct((n, seq_len, D_MODEL), F32) for n in (n_first, n_seq - n_first)]
    return pl.pallas_call(
        functools.partial(_out_proj_kernel, n_first=n_first),
        grid=(nj, n_seq),
        in_specs=[
            pl.BlockSpec((1, 1, n_ch, tb), lambda j, b: (j, b, 0, 0)),
            tok(D_CONF), tok(D_RG), tok(D_MIX), tok(D_MODEL),
            vec(D_MIX), pl.BlockSpec((D_MIX, D_MODEL), lambda j, b: (0, 0)), vec(D_MODEL),
        ],
        out_specs=out_specs,
        out_shape=out_shape,
        compiler_params=_params("arbitrary", "arbitrary"),
        name="out_proj",
    )(y_hy, y_cf, y_rg, gate, x, grp_g, w_out, final_g)


def _rg_gate_weights(wa, wx):
    per = LANES // GROUP_WIDTH
    n_cb = D_RG // LANES

    def diag(w):
        w = w.reshape(n_cb, per, GROUP_WIDTH, GROUP_WIDTH)
        eye = jnp.eye(per, dtype=w.dtype)
        return jnp.einsum("cpij,pq->cpiqj", w, eye).reshape(n_cb, LANES, LANES)

    dirs = [jnp.concatenate([diag(wa[d]), diag(wx[d])], axis=-1) for d in range(2)]
    return jnp.stack(dirs, axis=1).astype(BF16)


def _layer(x, n_first, final_g, norm_g, w_in, hy_conv_w, hy_conv_b, hy_w1, hy_b1, hy_w2, hy_b2, hy_w3, hy_b3,
           hy_w4, hy_freq, hy_d, cf_dw_w, cf_dw_b, cf_ln_g, cf_ln_b, cf_pw_w, cf_pw_b, rg_conv_w, rg_conv_b,
           rg_wa, rg_ba, rg_wx, rg_bx, rg_lam, grp_g, w_out):
    n_seq, seq_len, _ = x.shape
    s1 = 3 * D_HYENA
    g_row = norm_g.reshape(1, D_MODEL)

    kfull = _filters(hy_w1, hy_b1, hy_w2, hy_b2, hy_w3, hy_b3, hy_w4, hy_freq, seq_len)
    cw = jnp.concatenate([hy_conv_w, hy_conv_b[None]], axis=0)
    cw = jnp.broadcast_to(cw[:, :, None], (4, s1, LANES))
    u, x0 = _hy_proj(x, g_row, w_in[:, :s1].T.astype(BF16), cw)
    y_hy = _longconv(kfull, u, x0, hy_d)

    p_cf, p_rg, gate = _in_proj(x.reshape(n_seq * seq_len, D_MODEL), g_row, w_in[:, s1:].astype(BF16))
    zrow = jnp.zeros((D_CONF,), F32)
    cf_vec = jnp.stack([cf_dw_b, cf_ln_g, cf_ln_b, cf_pw_b, zrow, zrow, zrow, zrow])
    y_cf = _conformer(p_cf.reshape(n_seq, seq_len, 2 * D_CONF), cf_dw_w, cf_vec, cf_pw_w.astype(BF16))

    n_cb = D_RG // LANES
    zr = jnp.zeros((D_RG,), F32)
    rg_rows = [rg_conv_b, rg_ba[0], rg_bx[0], rg_lam[0], zr, rg_ba[1], rg_bx[1], rg_lam[1]] + [zr] * 8
    rg_vec = jnp.stack(rg_rows).reshape(2 * SUBLANES, n_cb, LANES).transpose(1, 0, 2)
    y_rg = _rglru(p_rg.reshape(n_seq, seq_len, D_RG), rg_conv_w, _rg_gate_weights(rg_wa, rg_wx), rg_vec)

    return _out_proj(y_hy, y_cf, y_rg, gate.reshape(n_seq, seq_len, D_MIX), x, grp_g.reshape(1, D_MIX),
                     w_out.astype(BF16), final_g.reshape(1, D_MODEL), n_first)


def kernel(x_prompt, x_sample, norm_g, w_in, hy_conv_w, hy_conv_b, hy_w1, hy_b1, hy_w2, hy_b2, hy_w3, hy_b3, hy_w4, hy_freq, hy_d, cf_dw_w, cf_dw_b, cf_ln_g, cf_ln_b, cf_pw_w, cf_pw_b, rg_conv_w, rg_conv_b, rg_wa, rg_ba, rg_wx, rg_bx, rg_lam, grp_g, w_out, final_g):
    layer_params = (norm_g, w_in, hy_conv_w, hy_conv_b, hy_w1, hy_b1, hy_w2, hy_b2, hy_w3, hy_b3, hy_w4, hy_freq,
                    hy_d, cf_dw_w, cf_dw_b, cf_ln_g, cf_ln_b, cf_pw_w, cf_pw_b, rg_conv_w, rg_conv_b, rg_wa,
                    rg_ba, rg_wx, rg_bx, rg_lam, grp_g, w_out)
    assert x_prompt.shape[1:] == x_sample.shape[1:]
    n_prompt = x_prompt.shape[0]
    assert n_prompt + x_sample.shape[0] <= BATCH_PAD
    depth = norm_g.shape[0]
    x = jnp.concatenate([x_prompt, x_sample], axis=0)
    for l in range(depth - 1):
        (x,) = _layer(x, None, final_g, *[w[l] for w in layer_params])
    y_prompt, y_sample = _layer(x, n_prompt, final_g, *[w[depth - 1] for w in layer_params])
    return (y_prompt, y_sample)
```

```python
import functools
import math

import jax
import jax.numpy as jnp
from jax import lax
from jax.experimental import pallas as pl
from jax.experimental.pallas import tpu as pltpu

F32 = jnp.float32
BF16 = jnp.bfloat16

D_MODEL = 1024
D_HYENA = 384
D_CONF = 256
D_RG = 384
D_MIX = D_HYENA + D_CONF + D_RG
GROUP_WIDTH = 64
HYENA_BANDS = 16
HYENA_ORDER = 64
HYENA_TARGET = 1e-2
HYENA_FAST_DECAY_PCT = 0.3
HYENA_SLOW_DECAY_PCT = 1.5
CONF_K = 31
RG_C = 8.0
EPS = 1e-6

LANES = 128
SUBLANES = 8
MXU_DIM = 256
VMEM_LIMIT_BYTES = 56 * 1024 * 1024

TOEPLITZ_BLOCK = MXU_DIM
ROW_TILE = 16
FILTER_TILE = 2048
PROJ_ROWS = 512
CONF_ROWS = 512
CONF_HALO = 16
CONF_SUB = 64
OUT_ROWS = 512
RG_CHUNK = 256
RG_PAD = 8


def _params(*sem):
    return pltpu.CompilerParams(dimension_semantics=sem, vmem_limit_bytes=VMEM_LIMIT_BYTES)


def _rms(x, g):
    return x * lax.rsqrt(jnp.mean(x * x, axis=-1, keepdims=True) + EPS) * g


def _filters_kernel(w1t_ref, w1c_ref, w1s_ref, b1_ref, w2_ref, b2_ref, w3_ref, b3_ref, w4_ref,
                    fr_ref, o_ref, *, seq_len):
    i = pl.program_id(0)
    tile = o_ref.shape[1]
    m = i * tile + lax.broadcasted_iota(jnp.int32, (1, tile), 1)
    pos = jnp.abs(m - seq_len).astype(F32)
    t = pos / (seq_len - 1.0)
    w = (2.0 * math.pi / seq_len) * pos
    band = lax.broadcasted_iota(jnp.int32, (HYENA_BANDS, 1), 0).astype(F32)
    f = 1e-4 + band * ((HYENA_BANDS - 1 - 1e-4) / (HYENA_BANDS - 1))
    fw = f * w
    hi = lax.Precision.HIGHEST
    fr = fr_ref[...]
    pre = (w1t_ref[...] * t + jnp.dot(w1c_ref[...], jnp.cos(fw), precision=hi)
           - jnp.dot(w1s_ref[...], jnp.sin(fw), precision=hi))
    h = jnp.sin(fr * (pre + b1_ref[...]))
    h = jnp.sin(fr * (jnp.dot(w2_ref[...], h, precision=hi) + b2_ref[...]))
    h = jnp.sin(fr * (jnp.dot(w3_ref[...], h, precision=hi) + b3_ref[...]))
    k = jnp.dot(w4_ref[...], h, precision=hi)
    max_decay = math.log(HYENA_TARGET) / HYENA_FAST_DECAY_PCT
    min_decay = math.log(HYENA_TARGET) / HYENA_SLOW_DECAY_PCT
    ch = lax.broadcasted_iota(jnp.int32, (D_HYENA, 1), 0).astype(F32)
    delta = jnp.abs(min_decay + ch * ((max_decay - min_decay) / (D_HYENA - 1)))
    decay = jnp.exp(-t * delta)
    k = jnp.where(m >= seq_len, k[:D_HYENA], k[D_HYENA:]) * decay
    o_ref[...] = jnp.where(m == 0, 0.0, k)


def _filters(w1, b1, w2, b2, w3, b3, w4, freq, seq_len):
    w1t = w1.T
    col = lambda v: v.reshape(-1, 1)
    args = (w1t[:, 0:1], w1t[:, 1:1 + HYENA_BANDS], w1t[:, 1 + HYENA_BANDS:], col(b1), w2.T, col(b2),
            w3.T, col(b3), w4.T, col(freq))
    full = lambda a: pl.BlockSpec(a.shape, lambda i: (0, 0))
    return pl.pallas_call(
        functools.partial(_filters_kernel, seq_len=seq_len),
        grid=(2 * seq_len // FILTER_TILE,),
        in_specs=[full(a) for a in args],
        out_specs=pl.BlockSpec((D_HYENA, FILTER_TILE), lambda i: (0, i)),
        out_shape=jax.ShapeDtypeStruct((D_HYENA, 2 * seq_len), F32),
        compiler_params=_params("arbitrary"),
        name="filters",
    )(*args)


def _hy_proj_kernel(x_ref, g_ref, w_ref, cw_ref, u_ref, x0_ref, p_ref, l_ref, s_ref):
    j = pl.program_id(1)
    nj = pl.num_programs(1) - 1
    half = LANES

    @pl.when(j == 0)
    def _():
        p_ref[...] = jnp.zeros(p_ref.shape, F32)
        l_ref[...] = jnp.zeros(l_ref.shape, F32)
        s_ref[...] = jnp.zeros(s_ref.shape, F32)

    xn = _rms(x_ref[0], g_ref[...]).astype(BF16)
    p_cur = lax.dot_general(w_ref[...], xn, (((1,), (1,)), ((), ())), preferred_element_type=F32)
    s_cur = pltpu.roll(p_cur[:, :half], half - 1, 1)

    lane = lax.broadcasted_iota(jnp.int32, (1, half), 1)
    p_lo = p_ref[:, :half]
    p_hi = p_ref[:, half:]
    r_lo = pltpu.roll(p_lo, 1, 1)
    r_hi = pltpu.roll(p_hi, 1, 1)
    s_hi = pltpu.roll(p_hi, half - 1, 1)
    s_right = jnp.where(j < nj, s_cur, 0.0)
    first = lane == 0
    last = lane == half - 1
    halves = (
        (jnp.where(first, l_ref[...], r_lo), p_lo, jnp.where(last, s_hi, s_ref[...])),
        (jnp.where(first, r_lo, r_hi), p_hi, jnp.where(last, s_right, s_hi)),
    )
    for h, (pm1, p0, pp1) in enumerate(halves):
        uc = cw_ref[0] * pm1 + cw_ref[1] * p0 + cw_ref[2] * pp1 + cw_ref[3]
        cols = slice(h * half, (h + 1) * half)
        x0_ref[0, 0, :, cols] = uc[:D_HYENA]
        u_ref[0, 0, :, cols] = uc[2 * D_HYENA:] * uc[D_HYENA:2 * D_HYENA]
    l_ref[...] = r_hi
    p_ref[...] = p_cur
    s_ref[...] = s_cur


def _hy_proj(x, norm_g, w_hy_t, cw):
    n_seq, seq_len, _ = x.shape
    tb = TOEPLITZ_BLOCK
    nj = seq_len // tb
    out = jax.ShapeDtypeStruct((nj, n_seq, D_HYENA, tb), F32)
    out_spec = pl.BlockSpec((1, 1, D_HYENA, tb), lambda b, j: (jnp.maximum(j - 1, 0), b, 0, 0))
    return pl.pallas_call(
        _hy_proj_kernel,
        grid=(n_seq, nj + 1),
        in_specs=[
            pl.BlockSpec((1, tb, D_MODEL), lambda b, j: (b, jnp.minimum(j, nj - 1), 0)),
            pl.BlockSpec((1, D_MODEL), lambda b, j: (0, 0)),
            pl.BlockSpec((3 * D_HYENA, D_MODEL), lambda b, j: (0, 0)),
            pl.BlockSpec((4, 3 * D_HYENA, LANES), lambda b, j: (0, 0, 0)),
        ],
        out_specs=[out_spec, out_spec],
        out_shape=[out, out],
        scratch_shapes=[pltpu.VMEM((3 * D_HYENA, tb), F32), pltpu.VMEM((3 * D_HYENA, LANES), F32),
                        pltpu.VMEM((3 * D_HYENA, LANES), F32)],
        compiler_params=_params("arbitrary", "arbitrary"),
        name="hy_proj",
    )(x, norm_g, w_hy_t, cw)


def _longconv_kernel(d_ref, k_ref, u_ref, x0_ref, o_ref, s_ref, ub_ref, y_ref, *, n_seq):
    c = pl.program_id(0)
    tb = TOEPLITZ_BLOCK
    half = tb // 2
    rows, n_ch, _ = u_ref.shape
    nj = rows // n_seq
    seq_len = nj * tb
    step = math.gcd(n_seq, ROW_TILE)
    n_phase = ROW_TILE // step

    tri = (lax.broadcasted_iota(jnp.int32, (half, half), 1) >= lax.broadcasted_iota(jnp.int32, (half, half), 0))
    s_ref[rows:, :] = jnp.zeros((2 * ROW_TILE, tb), F32)

    def channel(i, carry):
        def circ(q):
            seg = jnp.broadcast_to(k_ref[i, :, half * q:half * (q + 1)], (half, half))
            return pltpu.roll(seg, 0, 1, stride=1, stride_axis=0)

        u = u_ref[:, pl.ds(i, 1), :][:, 0, :]
        s_ref[0:rows, :] = u
        for p in range(n_phase):
            ub_ref[p] = s_ref[p * step:p * step + rows + ROW_TILE, :].astype(BF16)
        y_ref[...] = jnp.zeros(y_ref.shape, F32)
        y_ref[0, ROW_TILE:ROW_TILE + rows, :] = u * d_ref[c * n_ch + i]

        q = (seq_len - tb * (nj - 1)) // half
        c_prev = circ(q - 1)
        g_prev = jnp.where(tri, c_prev, circ(q - 2)).astype(BF16)
        for d in range(-(nj - 1), nj):
            c_a = circ(q)
            c_b = circ(q + 1)
            g_a = jnp.where(tri, c_a, c_prev).astype(BF16)
            g_b = jnp.where(tri, c_b, c_a).astype(BF16)
            tile = jnp.concatenate(
                [jnp.concatenate([g_a, g_b], axis=1), jnp.concatenate([g_prev, g_a], axis=1)], axis=0)
            c_prev, g_prev = c_b, g_b
            q += 2
            n = -(-n_seq * (nj - abs(d)) // ROW_TILE) * ROW_TILE
            src = 0 if d >= 0 else n_seq * -d
            dst = n_seq * d if d >= 0 else 0
            ps, pd = (src % ROW_TILE) // step, (dst % ROW_TILE) // step
            src -= ps * step
            dst += ROW_TILE - pd * step
            y_ref[pd, dst:dst + n, :] += jnp.dot(ub_ref[ps, src:src + n, :], tile, preferred_element_type=F32)

        y = y_ref[0, ROW_TILE:ROW_TILE + rows, :]
        for p in range(1, n_phase):
            y = y + y_ref[p, ROW_TILE - p * step:ROW_TILE - p * step + rows, :]
        o_ref[:, pl.ds(i, 1), :] = (y * x0_ref[:, pl.ds(i, 1), :][:, 0, :])[:, None, :]
        return carry

    lax.fori_loop(0, n_ch, channel, 0)


def _longconv(kfull, u, x0, d_skip):
    nj, n_seq, n_ch, tb = u.shape
    rows = nj * n_seq
    seq_len = nj * tb
    n_phase = ROW_TILE // math.gcd(n_seq, ROW_TILE)
    blk = pl.BlockSpec((rows, SUBLANES, tb), lambda c: (0, c, 0))
    out = pl.pallas_call(
        functools.partial(_longconv_kernel, n_seq=n_seq),
        grid=(n_ch // SUBLANES,),
        in_specs=[
            pl.BlockSpec(memory_space=pltpu.SMEM),
            pl.BlockSpec((SUBLANES, 1, 2 * seq_len), lambda c: (c, 0, 0)),
            blk, blk,
        ],
        out_specs=blk,
        out_shape=jax.ShapeDtypeStruct((rows, n_ch, tb), F32),
        scratch_shapes=[pltpu.VMEM((rows + 2 * ROW_TILE, tb), F32), pltpu.VMEM((n_phase, rows + ROW_TILE, tb), BF16),
                        pltpu.VMEM((n_phase, rows + 2 * ROW_TILE, tb), F32)],
        compiler_params=_params("arbitrary"),
        name="longconv",
    )(d_skip, kfull.reshape(n_ch, 1, 2 * seq_len), u.reshape(rows, n_ch, tb), x0.reshape(rows, n_ch, tb))
    return out.reshape(nj, n_seq, n_ch, tb)


def _in_proj_kernel(x_ref, g_ref, w_ref, cf_ref, rg_ref, gate_ref):
    xn = _rms(x_ref[...], g_ref[...]).astype(BF16)
    p = jnp.dot(xn, w_ref[...], preferred_element_type=F32).astype(BF16)
    cf_ref[...] = p[:, :2 * D_CONF]
    rg_ref[...] = p[:, 2 * D_CONF:2 * D_CONF + D_RG]
    gate_ref[...] = p[:, 2 * D_CONF + D_RG:]


def _in_proj(x2d, norm_g, w_rest):
    n_tok = x2d.shape[0]
    n_col = w_rest.shape[1]
    row = lambda w: pl.BlockSpec((PROJ_ROWS, w), lambda i: (i, 0))
    return pl.pallas_call(
        _in_proj_kernel,
        grid=(n_tok // PROJ_ROWS,),
        in_specs=[row(D_MODEL), pl.BlockSpec((1, D_MODEL), lambda i: (0, 0)),
                  pl.BlockSpec((D_MODEL, n_col), lambda i: (0, 0))],
        out_specs=[row(2 * D_CONF), row(D_RG), row(D_MIX)],
        out_shape=[jax.ShapeDtypeStruct((n_tok, w), BF16) for w in (2 * D_CONF, D_RG, D_MIX)],
        compiler_params=_params("arbitrary"),
        name="in_proj",
    )(x2d, norm_g, w_rest)


def _glu(p):
    p = p.astype(F32)
    return p[:, :D_CONF] * jax.nn.sigmoid(p[:, D_CONF:])


def _conformer_kernel(p_ref, pp_ref, pn_ref, dw_ref, vec_ref, pw_ref, o_ref, ext_ref):
    t = pl.program_id(1)
    rows = p_ref.shape[1]
    half = (CONF_K - 1) // 2
    n_ext = rows + 2 * CONF_HALO
    ext_ref[0, 0:CONF_HALO, :] = jnp.where(t > 0, _glu(pp_ref[0]), 0.0)
    ext_ref[0, CONF_HALO:CONF_HALO + rows, :] = _glu(p_ref[0])
    ext_ref[0, CONF_HALO + rows:, :] = jnp.where(t < pl.num_programs(1) - 1, _glu(pn_ref[0]), 0.0)
    for r in range(1, SUBLANES):
        ext_ref[r, 0:n_ext - SUBLANES, :] = ext_ref[0, r:r + n_ext - SUBLANES, :]
    vec = vec_ref[...]
    dw_b, ln_g, ln_b, pw_b = vec[0:1], vec[1:2], vec[2:3], vec[3:4]
    dw = dw_ref[...]
    for s in range(rows // CONF_SUB):
        acc = jnp.broadcast_to(dw_b, (CONF_SUB, D_CONF))
        for k in range(CONF_K):
            off = CONF_HALO - half + k
            start = s * CONF_SUB + off - off % SUBLANES
            acc = acc + dw[k:k + 1] * ext_ref[off % SUBLANES, start:start + CONF_SUB, :]
        mu = jnp.mean(acc, axis=-1, keepdims=True)
        xc = acc - mu
        var = jnp.mean(xc * xc, axis=-1, keepdims=True)
        y = xc * lax.rsqrt(var + EPS) * ln_g + ln_b
        y = y * jax.nn.sigmoid(y)
        o_ref[0, s * CONF_SUB:(s + 1) * CONF_SUB, :] = (
            jnp.dot(y.astype(BF16), pw_ref[...], preferred_element_type=F32) + pw_b).astype(o_ref.dtype)


def _conformer(p_cf, dw_w, vec, pw_w):
    n_seq, seq_len, _ = p_cf.shape
    per = CONF_ROWS // CONF_HALO
    last = seq_len // CONF_HALO - 1
    return pl.pallas_call(
        _conformer_kernel,
        grid=(n_seq, seq_len // CONF_ROWS),
        in_specs=[
            pl.BlockSpec((1, CONF_ROWS, 2 * D_CONF), lambda b, t: (b, t, 0)),
            pl.BlockSpec((1, CONF_HALO, 2 * D_CONF), lambda b, t: (b, jnp.maximum(t * per - 1, 0), 0)),
            pl.BlockSpec((1, CONF_HALO, 2 * D_CONF), lambda b, t: (b, jnp.minimum((t + 1) * per, last), 0)),
            pl.BlockSpec((CONF_K, D_CONF), lambda b, t: (0, 0)),
            pl.BlockSpec((SUBLANES, D_CONF), lambda b, t: (0, 0)),
            pl.BlockSpec((D_CONF, D_CONF), lambda b, t: (0, 0)),
        ],
        out_specs=pl.BlockSpec((1, CONF_ROWS, D_CONF), lambda b, t: (b, t, 0)),
        out_shape=jax.ShapeDtypeStruct((n_seq, seq_len, D_CONF), BF16),
        scratch_shapes=[pltpu.VMEM((SUBLANES, CONF_ROWS + 2 * CONF_HALO, D_CONF), F32)],
        compiler_params=_params("arbitrary", "arbitrary"),
        name="conformer",
    )(p_cf, p_cf, p_cf, dw_w, vec, pw_w)


def _shift_sub(x, s, fill, reverse):
    r = x.shape[1]
    sub = lax.broadcasted_iota(jnp.int32, (1, r, 1), 1)
    if reverse:
        return jnp.where(sub < r - s, pltpu.roll(x, r - s, 1), fill)
    return jnp.where(sub >= s, pltpu.roll(x, s, 1), fill)


def _scan_chunk(a, b, carry, reverse):
    rows, lanes = a.shape
    n = rows // SUBLANES
    a = a.reshape(n, SUBLANES, lanes)
    b = b.reshape(n, SUBLANES, lanes)
    s = 1
    while s < SUBLANES:
        b = b + a * _shift_sub(b, s, 0.0, reverse)
        a = a * _shift_sub(a, s, 1.0, reverse)
        s *= 2
    edge = 0 if reverse else SUBLANES - 1
    at = jnp.broadcast_to(a[:, edge:edge + 1, :], a.shape)
    bt = jnp.broadcast_to(b[:, edge:edge + 1, :], b.shape)
    h_in = [None] * n
    for v in (range(n - 1, -1, -1) if reverse else range(n)):
        h_in[v] = carry
        carry = bt[v] + at[v] * carry
    h = b + a * jnp.stack(h_in)
    return h.reshape(rows, lanes), carry


def _rglru_kernel(p_ref, cw_ref, w_ref, vec_ref, o_ref, xp_ref, xr_ref, hf_ref):
    seq_len = p_ref.shape[1]
    lanes = p_ref.shape[2]
    n_chunk = seq_len // RG_CHUNK
    zeros = jnp.zeros((RG_PAD, lanes), F32)
    xp_ref[0:RG_PAD, :] = zeros
    xp_ref[RG_PAD + seq_len:, :] = zeros
    xp_ref[RG_PAD:RG_PAD + seq_len, :] = p_ref[0].astype(F32)
    cw = cw_ref[...]
    vec = vec_ref[0]
    conv_b = vec[0:1]

    def gates(xr, d):
        g = jnp.dot(xr.astype(BF16), w_ref[0, d], preferred_element_type=F32)
        r = jax.nn.sigmoid(g[:, :lanes] + vec[1 + 4 * d:2 + 4 * d])
        i = jax.nn.sigmoid(g[:, lanes:] + vec[2 + 4 * d:3 + 4 * d])
        sp = jax.nn.softplus(-vec[3 + 4 * d:4 + 4 * d])
        a = jnp.exp(-RG_C * r * sp)
        return a, jnp.sqrt(1.0 - a * a) * (i * xr)

    def fwd(c, carry):
        t0 = pl.multiple_of(c * RG_CHUNK, RG_CHUNK)
        x = xp_ref[pl.ds(t0, RG_CHUNK + 2 * RG_PAD), :]
        xr = conv_b
        for k in range(4):
            xr = xr + cw[k:k + 1] * x[RG_PAD - 2 + k:RG_PAD - 2 + k + RG_CHUNK]
        xr_ref[pl.ds(t0, RG_CHUNK), :] = xr
        a, bt = gates(xr, 0)
        h, carry = _scan_chunk(a, bt, carry, False)
        hf_ref[pl.ds(t0, RG_CHUNK), :] = h
        return carry

    lax.fori_loop(0, n_chunk, fwd, jnp.zeros((SUBLANES, lanes), F32), unroll=2)

    def bwd(i, carry):
        t0 = pl.multiple_of((n_chunk - 1 - i) * RG_CHUNK, RG_CHUNK)
        a, bt = gates(xr_ref[pl.ds(t0, RG_CHUNK), :], 1)
        h, carry = _scan_chunk(a, bt, carry, True)
        o_ref[0, pl.ds(t0, RG_CHUNK), :] = (hf_ref[pl.ds(t0, RG_CHUNK), :] + h).astype(o_ref.dtype)
        return carry

    lax.fori_loop(0, n_chunk, bwd, jnp.zeros((SUBLANES, lanes), F32), unroll=2)


def _rglru(p_rg, conv_w, w_cat, vec):
    n_seq, seq_len, _ = p_rg.shape
    n_cb = D_RG // LANES
    blk = pl.BlockSpec((1, seq_len, LANES), lambda b, c: (b, 0, c))
    return pl.pallas_call(
        _rglru_kernel,
        grid=(n_seq, n_cb),
        in_specs=[
            blk,
            pl.BlockSpec((4, LANES), lambda b, c: (0, c)),
            pl.BlockSpec((1, 2, LANES, 2 * LANES), lambda b, c: (c, 0, 0, 0)),
            pl.BlockSpec((1, 2 * SUBLANES, LANES), lambda b, c: (c, 0, 0)),
        ],
        out_specs=blk,
        out_shape=jax.ShapeDtypeStruct((n_seq, seq_len, D_RG), BF16),
        scratch_shapes=[pltpu.VMEM((seq_len + 2 * RG_PAD, LANES), F32), pltpu.VMEM((seq_len, LANES), F32),
                        pltpu.VMEM((seq_len, LANES), F32)],
        compiler_params=_params("arbitrary", "arbitrary"),
        name="rglru",
    )(p_rg, conv_w, w_cat, vec)


def _out_proj_kernel(yh_ref, ycf_ref, yrg_ref, gate_ref, x_ref, gg_ref, w_ref, fg_ref, *o_refs, n_first):
    b = pl.program_id(1)
    gg = gg_ref[...]
    yh = jnp.concatenate([yh_ref[k, 0].T for k in range(yh_ref.shape[0])], axis=0)
    y = jnp.concatenate([
        _rms(yh, gg[:, :D_HYENA]),
        _rms(ycf_ref[0].astype(F32), gg[:, D_HYENA:D_HYENA + D_CONF]),
        _rms(yrg_ref[0].astype(F32), gg[:, D_HYENA + D_CONF:]),
    ], axis=-1)
    gate = gate_ref[0].astype(F32)
    y = y * (gate * jax.nn.sigmoid(gate))
    out = x_ref[0] + jnp.dot(y.astype(BF16), w_ref[...], preferred_element_type=F32)
    if len(o_refs) == 1:
        o_refs[0][0] = out
        return
    out = _rms(out, fg_ref[...])

    @pl.when(b < n_first)
    def _():
        o_refs[0][0] = out

    @pl.when(b >= n_first)
    def _():
        o_refs[1][0] = out


def _out_proj(y_hy, y_cf, y_rg, gate, x, grp_g, w_out, final_g, n_first):
    n_seq, seq_len, _ = x.shape
    nj, _, n_ch, tb = y_hy.shape
    per = OUT_ROWS // tb
    tok = lambda w: pl.BlockSpec((1, OUT_ROWS, w), lambda j, b: (b, j, 0))
    vec = lambda w: pl.BlockSpec((1, w), lambda j, b: (0, 0))
    if n_first is None:
        out_specs = [tok(D_MODEL)]
        out_shape = [jax.ShapeDtypeStruct((n_seq, seq_len, D_MODEL), F32)]
    else:
        out_specs = [
            pl.BlockSpec((1, OUT_ROWS, D_MODEL), lambda j, b: (jnp.minimum(b, n_first - 1), j, 0)),
            pl.BlockSpec((1, OUT_ROWS, D_MODEL), lambda j, b: (jnp.maximum(b - n_first, 0), j, 0)),
        ]
        out_shape = [jax.ShapeDtypeStruct((n, seq_len, D_MODEL), F32) for n in (n_first, n_seq - n_first)]
    return pl.pallas_call(
        functools.partial(_out_proj_kernel, n_first=n_first),
        grid=(nj // per, n_seq),
        in_specs=[
            pl.BlockSpec((per, 1, n_ch, tb), lambda j, b: (j, b, 0, 0)),
            tok(D_CONF), tok(D_RG), tok(D_MIX), tok(D_MODEL),
            vec(D_MIX), pl.BlockSpec((D_MIX, D_MODEL), lambda j, b: (0, 0)), vec(D_MODEL),
        ],
        out_specs=out_specs,
        out_shape=out_shape,
        compiler_params=_params("arbitrary", "arbitrary"),
        name="out_proj",
    )(y_hy, y_cf, y_rg, gate, x, grp_g, w_out, final_g)


def _rg_gate_weights(wa, wx):
    per = LANES // GROUP_WIDTH
    n_cb = D_RG // LANES

    def diag(w):
        w = w.reshape(n_cb, per, GROUP_WIDTH, GROUP_WIDTH)
        eye = jnp.eye(per, dtype=w.dtype)
        return jnp.einsum("cpij,pq->cpiqj", w, eye).reshape(n_cb, LANES, LANES)

    dirs = [jnp.concatenate([diag(wa[d]), diag(wx[d])], axis=-1) for d in range(2)]
    return jnp.stack(dirs, axis=1).astype(BF16)


def _layer(x, n_first, final_g, norm_g, w_in, hy_conv_w, hy_conv_b, hy_w1, hy_b1, hy_w2, hy_b2, hy_w3, hy_b3,
           hy_w4, hy_freq, hy_d, cf_dw_w, cf_dw_b, cf_ln_g, cf_ln_b, cf_pw_w, cf_pw_b, rg_conv_w, rg_conv_b,
           rg_wa, rg_ba, rg_wx, rg_bx, rg_lam, grp_g, w_out):
    n_seq, seq_len, _ = x.shape
    s1 = 3 * D_HYENA
    g_row = norm_g.reshape(1, D_MODEL)

    kfull = _filters(hy_w1, hy_b1, hy_w2, hy_b2, hy_w3, hy_b3, hy_w4, hy_freq, seq_len)
    cw = jnp.concatenate([hy_conv_w, hy_conv_b[None]], axis=0)
    cw = jnp.broadcast_to(cw[:, :, None], (4, s1, LANES))
    u, x0 = _hy_proj(x, g_row, w_in[:, :s1].T.astype(BF16), cw)
    y_hy = _longconv(kfull, u, x0, hy_d)

    p_cf, p_rg, gate = _in_proj(x.reshape(n_seq * seq_len, D_MODEL), g_row, w_in[:, s1:].astype(BF16))
    zrow = jnp.zeros((D_CONF,), F32)
    cf_vec = jnp.stack([cf_dw_b, cf_ln_g, cf_ln_b, cf_pw_b, zrow, zrow, zrow, zrow])
    y_cf = _conformer(p_cf.reshape(n_seq, seq_len, 2 * D_CONF), cf_dw_w, cf_vec, cf_pw_w.astype(BF16))

    n_cb = D_RG // LANES
    zr = jnp.zeros((D_RG,), F32)
    rg_rows = [rg_conv_b, rg_ba[0], rg_bx[0], rg_lam[0], zr, rg_ba[1], rg_bx[1], rg_lam[1]] + [zr] * 8
    rg_vec = jnp.stack(rg_rows).reshape(2 * SUBLANES, n_cb, LANES).transpose(1, 0, 2)
    y_rg = _rglru(p_rg.reshape(n_seq, seq_len, D_RG), rg_conv_w, _rg_gate_weights(rg_wa, rg_wx), rg_vec)

    return _out_proj(y_hy, y_cf, y_rg, gate.reshape(n_seq, seq_len, D_MIX), x, grp_g.reshape(1, D_MIX),
                     w_out.astype(BF16), final_g.reshape(1, D_MODEL), n_first)


def kernel(x_prompt, x_sample, norm_g, w_in, hy_conv_w, hy_conv_b, hy_w1, hy_b1, hy_w2, hy_b2, hy_w3, hy_b3, hy_w4, hy_freq, hy_d, cf_dw_w, cf_dw_b, cf_ln_g, cf_ln_b, cf_pw_w, cf_pw_b, rg_conv_w, rg_conv_b, rg_wa, rg_ba, rg_wx, rg_bx, rg_lam, grp_g, w_out, final_g):
    layer_params = (norm_g, w_in, hy_conv_w, hy_conv_b, hy_w1, hy_b1, hy_w2, hy_b2, hy_w3, hy_b3, hy_w4, hy_freq,
                    hy_d, cf_dw_w, cf_dw_b, cf_ln_g, cf_ln_b, cf_pw_w, cf_pw_b, rg_conv_w, rg_conv_b, rg_wa,
                    rg_ba, rg_wx, rg_bx, rg_lam, grp_g, w_out)
    assert x_prompt.shape[1:] == x_sample.shape[1:]
    n_prompt = x_prompt.shape[0]
    n_rows = (n_prompt + x_sample.shape[0]) * (x_prompt.shape[1] // TOEPLITZ_BLOCK)
    assert n_rows % ROW_TILE == 0, "longconv needs (sequences x time blocks) to fill whole bf16 row tiles"
    depth = norm_g.shape[0]
    x = jnp.concatenate([x_prompt, x_sample], axis=0)
    for l in range(depth - 1):
        (x,) = _layer(x, None, final_g, *[w[l] for w in layer_params])
    y_prompt, y_sample = _layer(x, n_prompt, final_g, *[w[depth - 1] for w in layer_params])
    return (y_prompt, y_sample)
```

```python
import functools
import math

import jax
import jax.numpy as jnp
from jax import lax
from jax.experimental import pallas as pl
from jax.experimental.pallas import tpu as pltpu

F32 = jnp.float32
BF16 = jnp.bfloat16

D_MODEL = 1024
D_HYENA = 384
D_CONF = 256
D_RG = 384
D_MIX = D_HYENA + D_CONF + D_RG
GROUP_WIDTH = 64
HYENA_BANDS = 16
HYENA_ORDER = 64
HYENA_TARGET = 1e-2
HYENA_FAST_DECAY_PCT = 0.3
HYENA_SLOW_DECAY_PCT = 1.5
CONF_K = 31
RG_C = 8.0
EPS = 1e-6

LANES = 128
SUBLANES = 8
MXU_DIM = 256
VMEM_LIMIT_BYTES = 56 * 1024 * 1024

TOEPLITZ_BLOCK = MXU_DIM
ROW_TILE = 16
FILTER_TILE = 2048
PROJ_ROWS = 512
CONF_ROWS = 512
CONF_HALO = 16
CONF_SUB = 64
OUT_ROWS = 512
HY_ROWS = 128
RG_CHUNK = 256
RG_PAD = 8


def _params(*sem):
    return pltpu.CompilerParams(dimension_semantics=sem, vmem_limit_bytes=VMEM_LIMIT_BYTES)


def _rms(x, g):
    return x * lax.rsqrt(jnp.mean(x * x, axis=-1, keepdims=True) + EPS) * g


def _x_tile(x_refs, b, n_first):
    if len(x_refs) == 1:
        return x_refs[0][0]
    return jnp.where(b < n_first, x_refs[0][0], x_refs[1][0])


def _x_specs(xs, rows, seq_tile, hold_tile):
    if len(xs) == 1:
        return [pl.BlockSpec((1, rows, D_MODEL), lambda *g: (*seq_tile(*g), 0))]
    n1 = xs[0].shape[0]
    last = xs[0].shape[1] // rows - 1

    def first(*g):
        b, t = seq_tile(*g)
        return (jnp.minimum(b, n1 - 1), t if hold_tile else jnp.where(b < n1, t, last), 0)

    def second(*g):
        b, t = seq_tile(*g)
        return (jnp.maximum(b - n1, 0), t if hold_tile else jnp.where(b < n1, 0, t), 0)

    return [pl.BlockSpec((1, rows, D_MODEL), first), pl.BlockSpec((1, rows, D_MODEL), second)]


def _filters_kernel(w1t_ref, w1c_ref, w1s_ref, b1_ref, w2_ref, b2_ref, w3_ref, b3_ref, w4_ref,
                    fr_ref, o_ref, *, seq_len):
    i = pl.program_id(0)
    tile = o_ref.shape[1]
    m = i * tile + lax.broadcasted_iota(jnp.int32, (1, tile), 1)
    pos = jnp.abs(m - seq_len).astype(F32)
    t = pos / (seq_len - 1.0)
    w = (2.0 * math.pi / seq_len) * pos
    band = lax.broadcasted_iota(jnp.int32, (HYENA_BANDS, 1), 0).astype(F32)
    f = 1e-4 + band * ((HYENA_BANDS - 1 - 1e-4) / (HYENA_BANDS - 1))
    fw = f * w
    hi = lax.Precision.HIGHEST
    fr = fr_ref[...]
    pre = (w1t_ref[...] * t + jnp.dot(w1c_ref[...], jnp.cos(fw), precision=hi)
           - jnp.dot(w1s_ref[...], jnp.sin(fw), precision=hi))
    h = jnp.sin(fr * (pre + b1_ref[...]))
    h = jnp.sin(fr * (jnp.dot(w2_ref[...], h, precision=hi) + b2_ref[...]))
    h = jnp.sin(fr * (jnp.dot(w3_ref[...], h, precision=hi) + b3_ref[...]))
    k = jnp.dot(w4_ref[...], h, precision=hi)
    max_decay = math.log(HYENA_TARGET) / HYENA_FAST_DECAY_PCT
    min_decay = math.log(HYENA_TARGET) / HYENA_SLOW_DECAY_PCT
    ch = lax.broadcasted_iota(jnp.int32, (D_HYENA, 1), 0).astype(F32)
    delta = jnp.abs(min_decay + ch * ((max_decay - min_decay) / (D_HYENA - 1)))
    decay = jnp.exp(-t * delta)
    k = jnp.where(m >= seq_len, k[:D_HYENA], k[D_HYENA:]) * decay
    o_ref[...] = jnp.where(m == 0, 0.0, k)


def _filters(w1, b1, w2, b2, w3, b3, w4, freq, seq_len):
    w1t = w1.T
    col = lambda v: v.reshape(-1, 1)
    args = (w1t[:, 0:1], w1t[:, 1:1 + HYENA_BANDS], w1t[:, 1 + HYENA_BANDS:], col(b1), w2.T, col(b2),
            w3.T, col(b3), w4.T, col(freq))
    full = lambda a: pl.BlockSpec(a.shape, lambda i: (0, 0))
    return pl.pallas_call(
        functools.partial(_filters_kernel, seq_len=seq_len),
        grid=(2 * seq_len // FILTER_TILE,),
        in_specs=[full(a) for a in args],
        out_specs=pl.BlockSpec((D_HYENA, FILTER_TILE), lambda i: (0, i)),
        out_shape=jax.ShapeDtypeStruct((D_HYENA, 2 * seq_len), F32),
        compiler_params=_params("arbitrary"),
        name="filters",
    )(*args)


def _hy_proj_kernel(*refs, n_x, n_first):
    x_refs = refs[:n_x]
    g_ref, w_ref, cw_ref, u_ref, x0_ref, p_ref, l_ref, s_ref = refs[n_x:]
    j = pl.program_id(1)
    nj = pl.num_programs(1) - 1
    half = LANES

    @pl.when(j == 0)
    def _():
        p_ref[...] = jnp.zeros(p_ref.shape, F32)
        l_ref[...] = jnp.zeros(l_ref.shape, F32)
        s_ref[...] = jnp.zeros(s_ref.shape, F32)

    xn = _rms(_x_tile(x_refs, pl.program_id(0), n_first), g_ref[...]).astype(BF16)
    p_cur = lax.dot_general(w_ref[...], xn, (((1,), (1,)), ((), ())), preferred_element_type=F32)
    s_cur = pltpu.roll(p_cur[:, :half], half - 1, 1)

    lane = lax.broadcasted_iota(jnp.int32, (1, half), 1)
    first = lane == 0
    last = lane == half - 1
    s_right = jnp.where(j < nj, s_cur, 0.0)

    def conv(rows):
        p_lo = p_ref[rows, :half]
        p_hi = p_ref[rows, half:]
        r_lo = pltpu.roll(p_lo, 1, 1)
        r_hi = pltpu.roll(p_hi, 1, 1)
        s_hi = pltpu.roll(p_hi, half - 1, 1)
        w0, w1, w2, bias = (cw_ref[k, rows, :] for k in range(4))
        lo = w0 * jnp.where(first, l_ref[rows, :], r_lo) + w1 * p_lo + w2 * jnp.where(last, s_hi, s_ref[rows, :]) + bias
        hi = w0 * jnp.where(first, r_lo, r_hi) + w1 * p_hi + w2 * jnp.where(last, s_right[rows], s_hi) + bias
        l_ref[rows, :] = r_hi
        return lo, hi

    for r in range(0, D_HYENA, HY_ROWS):
        x0 = conv(slice(r, r + HY_ROWS))
        x1 = conv(slice(D_HYENA + r, D_HYENA + r + HY_ROWS))
        v = conv(slice(2 * D_HYENA + r, 2 * D_HYENA + r + HY_ROWS))
        for h in range(2):
            cols = slice(h * half, (h + 1) * half)
            x0_ref[0, 0, r:r + HY_ROWS, cols] = x0[h]
            u_ref[0, 0, r:r + HY_ROWS, cols] = v[h] * x1[h]
    p_ref[...] = p_cur
    s_ref[...] = s_cur


def _hy_proj(xs, norm_g, w_hy_t, cw):
    n_seq = sum(x.shape[0] for x in xs)
    seq_len = xs[0].shape[1]
    tb = TOEPLITZ_BLOCK
    nj = seq_len // tb
    out = jax.ShapeDtypeStruct((nj, n_seq, D_HYENA, tb), F32)
    out_spec = pl.BlockSpec((1, 1, D_HYENA, tb), lambda b, j: (jnp.maximum(j - 1, 0), b, 0, 0))
    return pl.pallas_call(
        functools.partial(_hy_proj_kernel, n_x=len(xs), n_first=xs[0].shape[0]),
        grid=(n_seq, nj + 1),
        in_specs=_x_specs(xs, tb, lambda b, j: (b, jnp.minimum(j, nj - 1)), False) + [
            pl.BlockSpec((1, D_MODEL), lambda b, j: (0, 0)),
            pl.BlockSpec((3 * D_HYENA, D_MODEL), lambda b, j: (0, 0)),
            pl.BlockSpec((4, 3 * D_HYENA, LANES), lambda b, j: (0, 0, 0)),
        ],
        out_specs=[out_spec, out_spec],
        out_shape=[out, out],
        scratch_shapes=[pltpu.VMEM((3 * D_HYENA, tb), F32), pltpu.VMEM((3 * D_HYENA, LANES), F32),
                        pltpu.VMEM((3 * D_HYENA, LANES), F32)],
        compiler_params=_params("arbitrary", "arbitrary"),
        name="hy_proj",
    )(*xs, norm_g, w_hy_t, cw)


def _longconv_kernel(k_ref, u_ref, x0_ref, d_ref, o_ref, ut_ref, yt_ref, s_ref, ub_ref, y_ref, *, n_seq):
    tb = TOEPLITZ_BLOCK
    half = tb // 2
    rows, n_ch, _ = u_ref.shape
    nj = rows // n_seq
    seq_len = nj * tb
    step = math.gcd(n_seq, ROW_TILE)
    n_phase = ROW_TILE // step

    tri = (lax.broadcasted_iota(jnp.int32, (half, half), 1) >= lax.broadcasted_iota(jnp.int32, (half, half), 0))
    s_ref[rows:, :] = jnp.zeros((2 * ROW_TILE, tb), F32)
    ut_ref[...] = pltpu.einshape("rcl->crl", u_ref[...])

    def channel(i, carry):
        def circ(q):
            seg = jnp.broadcast_to(k_ref[i, :, half * q:half * (q + 1)], (half, half))
            return pltpu.roll(seg, 0, 1, stride=1, stride_axis=0)

        s_ref[0:rows, :] = ut_ref[i]
        for p in range(n_phase):
            ub_ref[p] = s_ref[p * step:p * step + rows + ROW_TILE, :].astype(BF16)
        y_ref[...] = jnp.zeros(y_ref.shape, F32)

        q = (seq_len - tb * (nj - 1)) // half
        c_prev = circ(q - 1)
        g_prev = jnp.where(tri, c_prev, circ(q - 2)).astype(BF16)
        for d in range(-(nj - 1), nj):
            c_a = circ(q)
            c_b = circ(q + 1)
            g_a = jnp.where(tri, c_a, c_prev).astype(BF16)
            g_b = jnp.where(tri, c_b, c_a).astype(BF16)
            tile = jnp.concatenate(
                [jnp.concatenate([g_a, g_b], axis=1), jnp.concatenate([g_prev, g_a], axis=1)], axis=0)
            c_prev, g_prev = c_b, g_b
            q += 2
            n = -(-n_seq * (nj - abs(d)) // ROW_TILE) * ROW_TILE
            src = 0 if d >= 0 else n_seq * -d
            dst = n_seq * d if d >= 0 else 0
            ps, pd = (src % ROW_TILE) // step, (dst % ROW_TILE) // step
            src -= ps * step
            dst += ROW_TILE - pd * step
            y_ref[pd, dst:dst + n, :] += jnp.dot(ub_ref[ps, src:src + n, :], tile, preferred_element_type=F32)

        y = y_ref[0, ROW_TILE:ROW_TILE + rows, :]
        for p in range(1, n_phase):
            y = y + y_ref[p, ROW_TILE - p * step:ROW_TILE - p * step + rows, :]
        yt_ref[i] = y
        return carry

    lax.fori_loop(0, n_ch, channel, 0, unroll=2)
    y = pltpu.einshape("crl->rcl", yt_ref[...])
    o_ref[...] = (y + u_ref[...] * d_ref[...][None]) * x0_ref[...]


def _longconv(kfull, u, x0, d_skip):
    nj, n_seq, n_ch, tb = u.shape
    rows = nj * n_seq
    seq_len = nj * tb
    n_phase = ROW_TILE // math.gcd(n_seq, ROW_TILE)
    blk = pl.BlockSpec((rows, SUBLANES, tb), lambda c: (0, c, 0))
    out = pl.pallas_call(
        functools.partial(_longconv_kernel, n_seq=n_seq),
        grid=(n_ch // SUBLANES,),
        in_specs=[
            pl.BlockSpec((SUBLANES, 1, 2 * seq_len), lambda c: (c, 0, 0)),
            blk, blk,
            pl.BlockSpec((SUBLANES, tb), lambda c: (c, 0)),
        ],
        out_specs=blk,
        out_shape=jax.ShapeDtypeStruct((rows, n_ch, tb), F32),
        scratch_shapes=[pltpu.VMEM((SUBLANES, rows, tb), F32), pltpu.VMEM((SUBLANES, rows, tb), F32),
                        pltpu.VMEM((rows + 2 * ROW_TILE, tb), F32), pltpu.VMEM((n_phase, rows + ROW_TILE, tb), BF16),
                        pltpu.VMEM((n_phase, rows + 2 * ROW_TILE, tb), F32)],
        compiler_params=_params("arbitrary"),
        name="longconv",
    )(kfull.reshape(n_ch, 1, 2 * seq_len), u.reshape(rows, n_ch, tb), x0.reshape(rows, n_ch, tb),
      jnp.broadcast_to(d_skip[:, None], (n_ch, tb)))
    return out.reshape(nj, n_seq, n_ch, tb)


def _in_proj_kernel(*refs, n_x, n_first):
    x_refs = refs[:n_x]
    g_ref, w_ref, cf_ref, rg_ref, gate_ref = refs[n_x:]
    xn = _rms(_x_tile(x_refs, pl.program_id(0), n_first), g_ref[...]).astype(BF16)
    p = jnp.dot(xn, w_ref[...], preferred_element_type=F32).astype(BF16)
    cf_ref[0] = p[:, :2 * D_CONF]
    rg_ref[0] = p[:, 2 * D_CONF:2 * D_CONF + D_RG]
    gate_ref[0] = p[:, 2 * D_CONF + D_RG:]


def _in_proj(xs, norm_g, w_rest):
    n_seq = sum(x.shape[0] for x in xs)
    seq_len = xs[0].shape[1]
    n_col = w_rest.shape[1]
    row = lambda w: pl.BlockSpec((1, PROJ_ROWS, w), lambda b, t: (b, t, 0))
    return pl.pallas_call(
        functools.partial(_in_proj_kernel, n_x=len(xs), n_first=xs[0].shape[0]),
        grid=(n_seq, seq_len // PROJ_ROWS),
        in_specs=_x_specs(xs, PROJ_ROWS, lambda b, t: (b, t), False) + [
            pl.BlockSpec((1, D_MODEL), lambda b, t: (0, 0)),
            pl.BlockSpec((D_MODEL, n_col), lambda b, t: (0, 0))],
        out_specs=[row(2 * D_CONF), row(D_RG), row(D_MIX)],
        out_shape=[jax.ShapeDtypeStruct((n_seq, seq_len, w), BF16) for w in (2 * D_CONF, D_RG, D_MIX)],
        compiler_params=_params("arbitrary", "arbitrary"),
        name="in_proj",
    )(*xs, norm_g, w_rest)


def _glu(p):
    p = p.astype(F32)
    return p[:, :D_CONF] * jax.nn.sigmoid(p[:, D_CONF:])


def _conformer_kernel(p_ref, pp_ref, pn_ref, dw_ref, vec_ref, pw_ref, o_ref, ext_ref):
    t = pl.program_id(1)
    rows = p_ref.shape[1]
    half = (CONF_K - 1) // 2
    n_ext = rows + 2 * CONF_HALO
    ext_ref[0, 0:CONF_HALO, :] = jnp.where(t > 0, _glu(pp_ref[0]), 0.0)
    ext_ref[0, CONF_HALO:CONF_HALO + rows, :] = _glu(p_ref[0])
    ext_ref[0, CONF_HALO + rows:, :] = jnp.where(t < pl.num_programs(1) - 1, _glu(pn_ref[0]), 0.0)
    for r in range(1, SUBLANES):
        ext_ref[r, 0:n_ext - SUBLANES, :] = ext_ref[0, r:r + n_ext - SUBLANES, :]
    vec = vec_ref[...]
    dw_b, ln_g, ln_b, pw_b = vec[0:1], vec[1:2], vec[2:3], vec[3:4]
    dw = dw_ref[...]
    for s in range(rows // CONF_SUB):
        acc = jnp.broadcast_to(dw_b, (CONF_SUB, D_CONF))
        for k in range(CONF_K):
            off = CONF_HALO - half + k
            start = s * CONF_SUB + off - off % SUBLANES
            acc = acc + dw[k:k + 1] * ext_ref[off % SUBLANES, start:start + CONF_SUB, :]
        mu = jnp.mean(acc, axis=-1, keepdims=True)
        xc = acc - mu
        var = jnp.mean(xc * xc, axis=-1, keepdims=True)
        y = xc * lax.rsqrt(var + EPS) * ln_g + ln_b
        y = y * jax.nn.sigmoid(y)
        o_ref[0, s * CONF_SUB:(s + 1) * CONF_SUB, :] = (
            jnp.dot(y.astype(BF16), pw_ref[...], preferred_element_type=F32) + pw_b).astype(o_ref.dtype)


def _conformer(p_cf, dw_w, vec, pw_w):
    n_seq, seq_len, _ = p_cf.shape
    per = CONF_ROWS // CONF_HALO
    last = seq_len // CONF_HALO - 1
    return pl.pallas_call(
        _conformer_kernel,
        grid=(n_seq, seq_len // CONF_ROWS),
        in_specs=[
            pl.BlockSpec((1, CONF_ROWS, 2 * D_CONF), lambda b, t: (b, t, 0)),
            pl.BlockSpec((1, CONF_HALO, 2 * D_CONF), lambda b, t: (b, jnp.maximum(t * per - 1, 0), 0)),
            pl.BlockSpec((1, CONF_HALO, 2 * D_CONF), lambda b, t: (b, jnp.minimum((t + 1) * per, last), 0)),
            pl.BlockSpec((CONF_K, D_CONF), lambda b, t: (0, 0)),
            pl.BlockSpec((SUBLANES, D_CONF), lambda b, t: (0, 0)),
            pl.BlockSpec((D_CONF, D_CONF), lambda b, t: (0, 0)),
        ],
        out_specs=pl.BlockSpec((1, CONF_ROWS, D_CONF), lambda b, t: (b, t, 0)),
        out_shape=jax.ShapeDtypeStruct((n_seq, seq_len, D_CONF), BF16),
        scratch_shapes=[pltpu.VMEM((SUBLANES, CONF_ROWS + 2 * CONF_HALO, D_CONF), F32)],
        compiler_params=_params("arbitrary", "arbitrary"),
        name="conformer",
    )(p_cf, p_cf, p_cf, dw_w, vec, pw_w)


def _shift_sub(x, s, fill, reverse):
    r = x.shape[1]
    sub = lax.broadcasted_iota(jnp.int32, (1, r, 1), 1)
    if reverse:
        return jnp.where(sub < r - s, pltpu.roll(x, r - s, 1), fill)
    return jnp.where(sub >= s, pltpu.roll(x, s, 1), fill)


def _scan_chunk(a, b, carry, reverse):
    rows, lanes = a.shape
    n = rows // SUBLANES
    a = a.reshape(n, SUBLANES, lanes)
    b = b.reshape(n, SUBLANES, lanes)
    s = 1
    while s < SUBLANES:
        b = b + a * _shift_sub(b, s, 0.0, reverse)
        a = a * _shift_sub(a, s, 1.0, reverse)
        s *= 2
    edge = 0 if reverse else SUBLANES - 1
    at = jnp.broadcast_to(a[:, edge:edge + 1, :], a.shape)
    bt = jnp.broadcast_to(b[:, edge:edge + 1, :], b.shape)
    h_in = [None] * n
    for v in (range(n - 1, -1, -1) if reverse else range(n)):
        h_in[v] = carry
        carry = bt[v] + at[v] * carry
    h = b + a * jnp.stack(h_in)
    return h.reshape(rows, lanes), carry


def _rglru_kernel(p_ref, cw_ref, w_ref, vec_ref, o_ref, xp_ref, xr_ref, hf_ref):
    seq_len = p_ref.shape[1]
    lanes = p_ref.shape[2]
    n_chunk = seq_len // RG_CHUNK
    zeros = jnp.zeros((RG_PAD, lanes), F32)
    xp_ref[0:RG_PAD, :] = zeros
    xp_ref[RG_PAD + seq_len:, :] = zeros
    xp_ref[RG_PAD:RG_PAD + seq_len, :] = p_ref[0].astype(F32)
    cw = cw_ref[...]
    vec = vec_ref[0]
    conv_b = vec[0:1]

    def gates(xr, d):
        g = jnp.dot(xr.astype(BF16), w_ref[0, d], preferred_element_type=F32)
        r = jax.nn.sigmoid(g[:, :lanes] + vec[1 + 4 * d:2 + 4 * d])
        i = jax.nn.sigmoid(g[:, lanes:] + vec[2 + 4 * d:3 + 4 * d])
        sp = jax.nn.softplus(-vec[3 + 4 * d:4 + 4 * d])
        a = jnp.exp(-RG_C * r * sp)
        return a, jnp.sqrt(1.0 - a * a) * (i * xr)

    def fwd(c, carry):
        t0 = pl.multiple_of(c * RG_CHUNK, RG_CHUNK)
        x = xp_ref[pl.ds(t0, RG_CHUNK + 2 * RG_PAD), :]
        xr = conv_b
        for k in range(4):
            xr = xr + cw[k:k + 1] * x[RG_PAD - 2 + k:RG_PAD - 2 + k + RG_CHUNK]
        xr_ref[pl.ds(t0, RG_CHUNK), :] = xr
        a, bt = gates(xr, 0)
        h, carry = _scan_chunk(a, bt, carry, False)
        hf_ref[pl.ds(t0, RG_CHUNK), :] = h
        return carry

    lax.fori_loop(0, n_chunk, fwd, jnp.zeros((SUBLANES, lanes), F32), unroll=2)

    def bwd(i, carry):
        t0 = pl.multiple_of((n_chunk - 1 - i) * RG_CHUNK, RG_CHUNK)
        a, bt = gates(xr_ref[pl.ds(t0, RG_CHUNK), :], 1)
        h, carry = _scan_chunk(a, bt, carry, True)
        o_ref[0, pl.ds(t0, RG_CHUNK), :] = (hf_ref[pl.ds(t0, RG_CHUNK), :] + h).astype(o_ref.dtype)
        return carry

    lax.fori_loop(0, n_chunk, bwd, jnp.zeros((SUBLANES, lanes), F32), unroll=2)


def _rglru(p_rg, conv_w, w_cat, vec):
    n_seq, seq_len, _ = p_rg.shape
    n_cb = D_RG // LANES
    blk = pl.BlockSpec((1, seq_len, LANES), lambda b, c: (b, 0, c))
    return pl.pallas_call(
        _rglru_kernel,
        grid=(n_seq, n_cb),
        in_specs=[
            blk,
            pl.BlockSpec((4, LANES), lambda b, c: (0, c)),
            pl.BlockSpec((1, 2, LANES, 2 * LANES), lambda b, c: (c, 0, 0, 0)),
            pl.BlockSpec((1, 2 * SUBLANES, LANES), lambda b, c: (c, 0, 0)),
        ],
        out_specs=blk,
        out_shape=jax.ShapeDtypeStruct((n_seq, seq_len, D_RG), BF16),
        scratch_shapes=[pltpu.VMEM((seq_len + 2 * RG_PAD, LANES), F32), pltpu.VMEM((seq_len, LANES), F32),
                        pltpu.VMEM((seq_len, LANES), F32)],
        compiler_params=_params("arbitrary", "arbitrary"),
        name="rglru",
    )(p_rg, conv_w, w_cat, vec)


def _out_proj_kernel(yh_ref, ycf_ref, yrg_ref, gate_ref, *refs, n_x, n_first):
    x_refs = refs[:n_x]
    gg_ref, w_ref, fg_ref = refs[n_x:n_x + 3]
    o_refs = refs[n_x + 3:]
    b = pl.program_id(1)
    gg = gg_ref[...]
    yh = jnp.concatenate([yh_ref[k, 0].T for k in range(yh_ref.shape[0])], axis=0)
    y = jnp.concatenate([
        _rms(yh, gg[:, :D_HYENA]),
        _rms(ycf_ref[0].astype(F32), gg[:, D_HYENA:D_HYENA + D_CONF]),
        _rms(yrg_ref[0].astype(F32), gg[:, D_HYENA + D_CONF:]),
    ], axis=-1)
    gate = gate_ref[0].astype(F32)
    y = y * (gate * jax.nn.sigmoid(gate))
    out = _x_tile(x_refs, b, n_first) + jnp.dot(y.astype(BF16), w_ref[...], preferred_element_type=F32)
    if len(o_refs) == 1:
        o_refs[0][0] = out
        return
    out = _rms(out, fg_ref[...])

    @pl.when(b < n_first)
    def _():
        o_refs[0][0] = out

    @pl.when(b >= n_first)
    def _():
        o_refs[1][0] = out


def _out_proj(y_hy, y_cf, y_rg, gate, xs, grp_g, w_out, final_g, n_first, split):
    nj, n_seq, n_ch, tb = y_hy.shape
    seq_len = nj * tb
    per = OUT_ROWS // tb
    tok = lambda w: pl.BlockSpec((1, OUT_ROWS, w), lambda j, b: (b, j, 0))
    vec = lambda w: pl.BlockSpec((1, w), lambda j, b: (0, 0))
    if not split:
        out_specs = [tok(D_MODEL)]
        out_shape = [jax.ShapeDtypeStruct((n_seq, seq_len, D_MODEL), F32)]
    else:
        out_specs = [
            pl.BlockSpec((1, OUT_ROWS, D_MODEL), lambda j, b: (jnp.minimum(b, n_first - 1), j, 0)),
            pl.BlockSpec((1, OUT_ROWS, D_MODEL), lambda j, b: (jnp.maximum(b - n_first, 0), j, 0)),
        ]
        out_shape = [jax.ShapeDtypeStruct((n, seq_len, D_MODEL), F32) for n in (n_first, n_seq - n_first)]
    return pl.pallas_call(
        functools.partial(_out_proj_kernel, n_x=len(xs), n_first=n_first),
        grid=(nj // per, n_seq),
        in_specs=[
            pl.BlockSpec((per, 1, n_ch, tb), lambda j, b: (j, b, 0, 0)),
            tok(D_CONF), tok(D_RG), tok(D_MIX),
        ] + _x_specs(xs, OUT_ROWS, lambda j, b: (b, j), True) + [
            vec(D_MIX), pl.BlockSpec((D_MIX, D_MODEL), lambda j, b: (0, 0)), vec(D_MODEL),
        ],
        out_specs=out_specs,
        out_shape=out_shape,
        compiler_params=_params("arbitrary", "arbitrary"),
        name="out_proj",
    )(y_hy, y_cf, y_rg, gate, *xs, grp_g, w_out, final_g)


def _rg_gate_weights(wa, wx):
    per = LANES // GROUP_WIDTH
    n_cb = D_RG // LANES

    def diag(w):
        w = w.reshape(n_cb, per, GROUP_WIDTH, GROUP_WIDTH)
        eye = jnp.eye(per, dtype=w.dtype)
        return jnp.einsum("cpij,pq->cpiqj", w, eye).reshape(n_cb, LANES, LANES)

    dirs = [jnp.concatenate([diag(wa[d]), diag(wx[d])], axis=-1) for d in range(2)]
    return jnp.stack(dirs, axis=1).astype(BF16)


def _layer(xs, n_first, split, final_g, norm_g, w_in, hy_conv_w, hy_conv_b, hy_w1, hy_b1, hy_w2, hy_b2, hy_w3,
           hy_b3, hy_w4, hy_freq, hy_d, cf_dw_w, cf_dw_b, cf_ln_g, cf_ln_b, cf_pw_w, cf_pw_b, rg_conv_w,
           rg_conv_b, rg_wa, rg_ba, rg_wx, rg_bx, rg_lam, grp_g, w_out):
    seq_len = xs[0].shape[1]
    s1 = 3 * D_HYENA
    g_row = norm_g.reshape(1, D_MODEL)

    kfull = _filters(hy_w1, hy_b1, hy_w2, hy_b2, hy_w3, hy_b3, hy_w4, hy_freq, seq_len)
    cw = jnp.concatenate([hy_conv_w, hy_conv_b[None]], axis=0)
    cw = jnp.broadcast_to(cw[:, :, None], (4, s1, LANES))
    u, x0 = _hy_proj(xs, g_row, w_in[:, :s1].T.astype(BF16), cw)
    y_hy = _longconv(kfull, u, x0, hy_d)

    p_cf, p_rg, gate = _in_proj(xs, g_row, w_in[:, s1:].astype(BF16))
    zrow = jnp.zeros((D_CONF,), F32)
    cf_vec = jnp.stack([cf_dw_b, cf_ln_g, cf_ln_b, cf_pw_b, zrow, zrow, zrow, zrow])
    y_cf = _conformer(p_cf, cf_dw_w, cf_vec, cf_pw_w.astype(BF16))

    n_cb = D_RG // LANES
    zr = jnp.zeros((D_RG,), F32)
    rg_rows = [rg_conv_b, rg_ba[0], rg_bx[0], rg_lam[0], zr, rg_ba[1], rg_bx[1], rg_lam[1]] + [zr] * 8
    rg_vec = jnp.stack(rg_rows).reshape(2 * SUBLANES, n_cb, LANES).transpose(1, 0, 2)
    y_rg = _rglru(p_rg, rg_conv_w, _rg_gate_weights(rg_wa, rg_wx), rg_vec)

    return _out_proj(y_hy, y_cf, y_rg, gate, xs, grp_g.reshape(1, D_MIX), w_out.astype(BF16),
                     final_g.reshape(1, D_MODEL), n_first, split)


def kernel(x_prompt, x_sample, norm_g, w_in, hy_conv_w, hy_conv_b, hy_w1, hy_b1, hy_w2, hy_b2, hy_w3, hy_b3, hy_w4, hy_freq, hy_d, cf_dw_w, cf_dw_b, cf_ln_g, cf_ln_b, cf_pw_w, cf_pw_b, rg_conv_w, rg_conv_b, rg_wa, rg_ba, rg_wx, rg_bx, rg_lam, grp_g, w_out, final_g):
    layer_params = (norm_g, w_in, hy_conv_w, hy_conv_b, hy_w1, hy_b1, hy_w2, hy_b2, hy_w3, hy_b3, hy_w4, hy_freq,
                    hy_d, cf_dw_w, cf_dw_b, cf_ln_g, cf_ln_b, cf_pw_w, cf_pw_b, rg_conv_w, rg_conv_b, rg_wa,
                    rg_ba, rg_wx, rg_bx, rg_lam, grp_g, w_out)
    assert x_prompt.shape[1:] == x_sample.shape[1:]
    n_prompt = x_prompt.shape[0]
    n_rows = (n_prompt + x_sample.shape[0]) * (x_prompt.shape[1] // TOEPLITZ_BLOCK)
    assert n_rows % ROW_TILE == 0, "longconv needs (sequences x time blocks) to fill whole bf16 row tiles"
    depth = norm_g.shape[0]
    xs = (x_prompt, x_sample)
    for l in range(depth):
        xs = tuple(_layer(xs, n_prompt, l == depth - 1, final_g, *[w[l] for w in layer_params]))
    return xs
```

```python
import functools
import math

import jax
import jax.numpy as jnp
from jax import lax
from jax.experimental import pallas as pl
from jax.experimental.pallas import tpu as pltpu

F32 = jnp.float32
BF16 = jnp.bfloat16

D_MODEL = 1024
D_HYENA = 384
D_CONF = 256
D_RG = 384
D_MIX = D_HYENA + D_CONF + D_RG
GROUP_WIDTH = 64
HYENA_BANDS = 16
HYENA_ORDER = 64
HYENA_TARGET = 1e-2
HYENA_FAST_DECAY_PCT = 0.3
HYENA_SLOW_DECAY_PCT = 1.5
CONF_K = 31
RG_C = 8.0
EPS = 1e-6

LANES = 128
SUBLANES = 8
MXU_DIM = 256
VMEM_LIMIT_BYTES = 56 * 1024 * 1024

TOEPLITZ_BLOCK = MXU_DIM
ROW_TILE = 16
FILTER_TILE = 2048
PROJ_ROWS = 512
CONF_ROWS = 512
CONF_HALO = 16
CONF_SUB = 64
OUT_ROWS = 512
HY_ROWS = 128
RG_CHUNK = 256
RG_PAD = 8


def _params(*sem):
    return pltpu.CompilerParams(dimension_semantics=sem, vmem_limit_bytes=VMEM_LIMIT_BYTES)


def _rms(x, g):
    return x * lax.rsqrt(jnp.mean(x * x, axis=-1, keepdims=True) + EPS) * g


def _x_tile(x_refs, b, n_first):
    if len(x_refs) == 1:
        return x_refs[0][0]
    return jnp.where(b < n_first, x_refs[0][0], x_refs[1][0])


def _x_specs(xs, rows, seq_tile, hold_tile):
    if len(xs) == 1:
        return [pl.BlockSpec((1, rows, D_MODEL), lambda *g: (*seq_tile(*g), 0))]
    n1 = xs[0].shape[0]
    last = xs[0].shape[1] // rows - 1

    def first(*g):
        b, t = seq_tile(*g)
        return (jnp.minimum(b, n1 - 1), t if hold_tile else jnp.where(b < n1, t, last), 0)

    def second(*g):
        b, t = seq_tile(*g)
        return (jnp.maximum(b - n1, 0), t if hold_tile else jnp.where(b < n1, 0, t), 0)

    return [pl.BlockSpec((1, rows, D_MODEL), first), pl.BlockSpec((1, rows, D_MODEL), second)]


def _filters_kernel(w1t_ref, w1c_ref, w1s_ref, b1_ref, w2_ref, b2_ref, w3_ref, b3_ref, w4_ref,
                    fr_ref, o_ref, *, seq_len):
    i = pl.program_id(0)
    tile = o_ref.shape[1]
    m = i * tile + lax.broadcasted_iota(jnp.int32, (1, tile), 1)
    pos = jnp.abs(m - seq_len).astype(F32)
    t = pos / (seq_len - 1.0)
    w = (2.0 * math.pi / seq_len) * pos
    band = lax.broadcasted_iota(jnp.int32, (HYENA_BANDS, 1), 0).astype(F32)
    f = 1e-4 + band * ((HYENA_BANDS - 1 - 1e-4) / (HYENA_BANDS - 1))
    fw = f * w
    hi = lax.Precision.HIGHEST
    fr = fr_ref[...]
    pre = (w1t_ref[...] * t + jnp.dot(w1c_ref[...], jnp.cos(fw), precision=hi)
           - jnp.dot(w1s_ref[...], jnp.sin(fw), precision=hi))
    h = jnp.sin(fr * (pre + b1_ref[...]))
    h = jnp.sin(fr * (jnp.dot(w2_ref[...], h, precision=hi) + b2_ref[...]))
    h = jnp.sin(fr * (jnp.dot(w3_ref[...], h, precision=hi) + b3_ref[...]))
    k = jnp.dot(w4_ref[...], h, precision=hi)
    max_decay = math.log(HYENA_TARGET) / HYENA_FAST_DECAY_PCT
    min_decay = math.log(HYENA_TARGET) / HYENA_SLOW_DECAY_PCT
    ch = lax.broadcasted_iota(jnp.int32, (D_HYENA, 1), 0).astype(F32)
    delta = jnp.abs(min_decay + ch * ((max_decay - min_decay) / (D_HYENA - 1)))
    decay = jnp.exp(-t * delta)
    k = jnp.where(m >= seq_len, k[:D_HYENA], k[D_HYENA:]) * decay
    o_ref[...] = jnp.where(m == 0, 0.0, k)


def _filters(w1, b1, w2, b2, w3, b3, w4, freq, seq_len):
    w1t = w1.T
    col = lambda v: v.reshape(-1, 1)
    args = (w1t[:, 0:1], w1t[:, 1:1 + HYENA_BANDS], w1t[:, 1 + HYENA_BANDS:], col(b1), w2.T, col(b2),
            w3.T, col(b3), w4.T, col(freq))
    full = lambda a: pl.BlockSpec(a.shape, lambda i: (0, 0))
    return pl.pallas_call(
        functools.partial(_filters_kernel, seq_len=seq_len),
        grid=(2 * seq_len // FILTER_TILE,),
        in_specs=[full(a) for a in args],
        out_specs=pl.BlockSpec((D_HYENA, FILTER_TILE), lambda i: (0, i)),
        out_shape=jax.ShapeDtypeStruct((D_HYENA, 2 * seq_len), F32),
        compiler_params=_params("arbitrary"),
        name="filters",
    )(*args)


def _hy_proj_kernel(xn_ref, w_ref, cw_ref, u_ref, x0_ref, p_ref, l_ref, s_ref):
    j = pl.program_id(1)
    nj = pl.num_programs(1) - 1
    half = LANES

    @pl.when(j == 0)
    def _():
        p_ref[...] = jnp.zeros(p_ref.shape, F32)
        l_ref[...] = jnp.zeros(l_ref.shape, F32)
        s_ref[...] = jnp.zeros(s_ref.shape, F32)

    p_cur = jnp.dot(w_ref[...], xn_ref[0], preferred_element_type=F32)
    s_cur = pltpu.roll(p_cur[:, :half], half - 1, 1)

    lane = lax.broadcasted_iota(jnp.int32, (1, half), 1)
    first = lane == 0
    last = lane == half - 1
    s_right = jnp.where(j < nj, s_cur, 0.0)

    def conv(rows):
        p_lo = p_ref[rows, :half]
        p_hi = p_ref[rows, half:]
        r_lo = pltpu.roll(p_lo, 1, 1)
        r_hi = pltpu.roll(p_hi, 1, 1)
        s_hi = pltpu.roll(p_hi, half - 1, 1)
        w0, w1, w2, bias = (cw_ref[k, rows, :] for k in range(4))
        lo = w0 * jnp.where(first, l_ref[rows, :], r_lo) + w1 * p_lo + w2 * jnp.where(last, s_hi, s_ref[rows, :]) + bias
        hi = w0 * jnp.where(first, r_lo, r_hi) + w1 * p_hi + w2 * jnp.where(last, s_right[rows], s_hi) + bias
        l_ref[rows, :] = r_hi
        return lo, hi

    for r in range(0, D_HYENA, HY_ROWS):
        x0 = conv(slice(r, r + HY_ROWS))
        x1 = conv(slice(D_HYENA + r, D_HYENA + r + HY_ROWS))
        v = conv(slice(2 * D_HYENA + r, 2 * D_HYENA + r + HY_ROWS))
        for h in range(2):
            cols = slice(h * half, (h + 1) * half)
            x0_ref[0, 0, r:r + HY_ROWS, cols] = x0[h]
            u_ref[0, 0, r:r + HY_ROWS, cols] = v[h] * x1[h]
    p_ref[...] = p_cur
    s_ref[...] = s_cur


def _hy_proj(xn, w_hy_t, cw):
    n_seq, _, seq_len = xn.shape
    tb = TOEPLITZ_BLOCK
    nj = seq_len // tb
    out = jax.ShapeDtypeStruct((nj, n_seq, D_HYENA, tb), F32)
    out_spec = pl.BlockSpec((1, 1, D_HYENA, tb), lambda b, j: (jnp.maximum(j - 1, 0), b, 0, 0))
    return pl.pallas_call(
        _hy_proj_kernel,
        grid=(n_seq, nj + 1),
        in_specs=[
            pl.BlockSpec((1, D_MODEL, tb), lambda b, j: (b, 0, jnp.minimum(j, nj - 1))),
            pl.BlockSpec((3 * D_HYENA, D_MODEL), lambda b, j: (0, 0)),
            pl.BlockSpec((4, 3 * D_HYENA, LANES), lambda b, j: (0, 0, 0)),
        ],
        out_specs=[out_spec, out_spec],
        out_shape=[out, out],
        scratch_shapes=[pltpu.VMEM((3 * D_HYENA, tb), F32), pltpu.VMEM((3 * D_HYENA, LANES), F32),
                        pltpu.VMEM((3 * D_HYENA, LANES), F32)],
        compiler_params=_params("arbitrary", "arbitrary"),
        name="hy_proj",
    )(xn, w_hy_t, cw)


def _longconv_kernel(k_ref, u_ref, x0_ref, d_ref, o_ref, ut_ref, yt_ref, s_ref, ub_ref, y_ref, *, n_seq):
    tb = TOEPLITZ_BLOCK
    half = tb // 2
    rows, n_ch, _ = u_ref.shape
    nj = rows // n_seq
    seq_len = nj * tb
    step = math.gcd(n_seq, ROW_TILE)
    n_phase = ROW_TILE // step

    tri = (lax.broadcasted_iota(jnp.int32, (half, half), 1) >= lax.broadcasted_iota(jnp.int32, (half, half), 0))
    s_ref[rows:, :] = jnp.zeros((2 * ROW_TILE, tb), F32)
    ut_ref[...] = pltpu.einshape("rcl->crl", u_ref[...])

    def channel(i, carry):
        def circ(q):
            seg = jnp.broadcast_to(k_ref[i, :, half * q:half * (q + 1)], (half, half))
            return pltpu.roll(seg, 0, 1, stride=1, stride_axis=0)

        s_ref[0:rows, :] = ut_ref[i]
        for p in range(n_phase):
            ub_ref[p] = s_ref[p * step:p * step + rows + ROW_TILE, :].astype(BF16)
        y_ref[...] = jnp.zeros(y_ref.shape, F32)

        q = (seq_len - tb * (nj - 1)) // half
        c_prev = circ(q - 1)
        g_prev = jnp.where(tri, c_prev, circ(q - 2)).astype(BF16)
        for d in range(-(nj - 1), nj):
            c_a = circ(q)
            c_b = circ(q + 1)
            g_a = jnp.where(tri, c_a, c_prev).astype(BF16)
            g_b = jnp.where(tri, c_b, c_a).astype(BF16)
            tile = jnp.concatenate(
                [jnp.concatenate([g_a, g_b], axis=1), jnp.concatenate([g_prev, g_a], axis=1)], axis=0)
            c_prev, g_prev = c_b, g_b
            q += 2
            n = -(-n_seq * (nj - abs(d)) // ROW_TILE) * ROW_TILE
            src = 0 if d >= 0 else n_seq * -d
            dst = n_seq * d if d >= 0 else 0
            ps, pd = (src % ROW_TILE) // step, (dst % ROW_TILE) // step
            src -= ps * step
            dst += ROW_TILE - pd * step
            y_ref[pd, dst:dst + n, :] += jnp.dot(ub_ref[ps, src:src + n, :], tile, preferred_element_type=F32)

        y = y_ref[0, ROW_TILE:ROW_TILE + rows, :]
        for p in range(1, n_phase):
            y = y + y_ref[p, ROW_TILE - p * step:ROW_TILE - p * step + rows, :]
        yt_ref[i] = y
        return carry

    lax.fori_loop(0, n_ch, channel, 0, unroll=2)
    y = pltpu.einshape("crl->rcl", yt_ref[...])
    o_ref[...] = (y + u_ref[...] * d_ref[...][None]) * x0_ref[...]


def _longconv(kfull, u, x0, d_skip):
    nj, n_seq, n_ch, tb = u.shape
    rows = nj * n_seq
    seq_len = nj * tb
    n_phase = ROW_TILE // math.gcd(n_seq, ROW_TILE)
    blk = pl.BlockSpec((rows, SUBLANES, tb), lambda c: (0, c, 0))
    out = pl.pallas_call(
        functools.partial(_longconv_kernel, n_seq=n_seq),
        grid=(n_ch // SUBLANES,),
        in_specs=[
            pl.BlockSpec((SUBLANES, 1, 2 * seq_len), lambda c: (c, 0, 0)),
            blk, blk,
            pl.BlockSpec((SUBLANES, tb), lambda c: (c, 0)),
        ],
        out_specs=blk,
        out_shape=jax.ShapeDtypeStruct((rows, n_ch, tb), F32),
        scratch_shapes=[pltpu.VMEM((SUBLANES, rows, tb), F32), pltpu.VMEM((SUBLANES, rows, tb), F32),
                        pltpu.VMEM((rows + 2 * ROW_TILE, tb), F32), pltpu.VMEM((n_phase, rows + ROW_TILE, tb), BF16),
                        pltpu.VMEM((n_phase, rows + 2 * ROW_TILE, tb), F32)],
        compiler_params=_params("arbitrary"),
        name="longconv",
    )(kfull.reshape(n_ch, 1, 2 * seq_len), u.reshape(rows, n_ch, tb), x0.reshape(rows, n_ch, tb),
      jnp.broadcast_to(d_skip[:, None], (n_ch, tb)))
    return out.reshape(nj, n_seq, n_ch, tb)


def _in_proj_kernel(*refs, n_x, n_first):
    x_refs = refs[:n_x]
    g_ref, w_ref, cf_ref, rg_ref, gate_ref, xn_ref = refs[n_x:]
    xn = _rms(_x_tile(x_refs, pl.program_id(0), n_first), g_ref[...])
    xn_ref[0] = xn.T.astype(BF16)
    p = jnp.dot(xn.astype(BF16), w_ref[...], preferred_element_type=F32).astype(BF16)
    cf_ref[0] = p[:, :2 * D_CONF]
    rg_ref[0] = p[:, 2 * D_CONF:2 * D_CONF + D_RG]
    gate_ref[0] = p[:, 2 * D_CONF + D_RG:]


def _in_proj(xs, norm_g, w_rest):
    n_seq = sum(x.shape[0] for x in xs)
    seq_len = xs[0].shape[1]
    n_col = w_rest.shape[1]
    row = lambda w: pl.BlockSpec((1, PROJ_ROWS, w), lambda b, t: (b, t, 0))
    return pl.pallas_call(
        functools.partial(_in_proj_kernel, n_x=len(xs), n_first=xs[0].shape[0]),
        grid=(n_seq, seq_len // PROJ_ROWS),
        in_specs=_x_specs(xs, PROJ_ROWS, lambda b, t: (b, t), False) + [
            pl.BlockSpec((1, D_MODEL), lambda b, t: (0, 0)),
            pl.BlockSpec((D_MODEL, n_col), lambda b, t: (0, 0))],
        out_specs=[row(2 * D_CONF), row(D_RG), row(D_MIX),
                   pl.BlockSpec((1, D_MODEL, PROJ_ROWS), lambda b, t: (b, 0, t))],
        out_shape=[jax.ShapeDtypeStruct((n_seq, seq_len, w), BF16) for w in (2 * D_CONF, D_RG, D_MIX)]
        + [jax.ShapeDtypeStruct((n_seq, D_MODEL, seq_len), BF16)],
        compiler_params=_params("arbitrary", "arbitrary"),
        name="in_proj",
    )(*xs, norm_g, w_rest)


def _glu(p):
    p = p.astype(F32)
    return p[:, :D_CONF] * jax.nn.sigmoid(p[:, D_CONF:])


def _conformer_kernel(p_ref, pp_ref, pn_ref, dw_ref, vec_ref, pw_ref, o_ref, ext_ref):
    t = pl.program_id(1)
    rows = p_ref.shape[1]
    half = (CONF_K - 1) // 2
    n_ext = rows + 2 * CONF_HALO
    ext_ref[0, 0:CONF_HALO, :] = jnp.where(t > 0, _glu(pp_ref[0]), 0.0)
    ext_ref[0, CONF_HALO:CONF_HALO + rows, :] = _glu(p_ref[0])
    ext_ref[0, CONF_HALO + rows:, :] = jnp.where(t < pl.num_programs(1) - 1, _glu(pn_ref[0]), 0.0)
    for r in range(1, SUBLANES):
        ext_ref[r, 0:n_ext - SUBLANES, :] = ext_ref[0, r:r + n_ext - SUBLANES, :]
    vec = vec_ref[...]
    dw_b, ln_g, ln_b, pw_b = vec[0:1], vec[1:2], vec[2:3], vec[3:4]
    dw = dw_ref[...]
    for s in range(rows // CONF_SUB):
        acc = jnp.broadcast_to(dw_b, (CONF_SUB, D_CONF))
        for k in range(CONF_K):
            off = CONF_HALO - half + k
            start = s * CONF_SUB + off - off % SUBLANES
            acc = acc + dw[k:k + 1] * ext_ref[off % SUBLANES, start:start + CONF_SUB, :]
        mu = jnp.mean(acc, axis=-1, keepdims=True)
        xc = acc - mu
        var = jnp.mean(xc * xc, axis=-1, keepdims=True)
        y = xc * lax.rsqrt(var + EPS) * ln_g + ln_b
        y = y * jax.nn.sigmoid(y)
        o_ref[0, s * CONF_SUB:(s + 1) * CONF_SUB, :] = (
            jnp.dot(y.astype(BF16), pw_ref[...], preferred_element_type=F32) + pw_b).astype(o_ref.dtype)


def _conformer(p_cf, dw_w, vec, pw_w):
    n_seq, seq_len, _ = p_cf.shape
    per = CONF_ROWS // CONF_HALO
    last = seq_len // CONF_HALO - 1
    return pl.pallas_call(
        _conformer_kernel,
        grid=(n_seq, seq_len // CONF_ROWS),
        in_specs=[
            pl.BlockSpec((1, CONF_ROWS, 2 * D_CONF), lambda b, t: (b, t, 0)),
            pl.BlockSpec((1, CONF_HALO, 2 * D_CONF), lambda b, t: (b, jnp.maximum(t * per - 1, 0), 0)),
            pl.BlockSpec((1, CONF_HALO, 2 * D_CONF), lambda b, t: (b, jnp.minimum((t + 1) * per, last), 0)),
            pl.BlockSpec((CONF_K, D_CONF), lambda b, t: (0, 0)),
            pl.BlockSpec((SUBLANES, D_CONF), lambda b, t: (0, 0)),
            pl.BlockSpec((D_CONF, D_CONF), lambda b, t: (0, 0)),
        ],
        out_specs=pl.BlockSpec((1, CONF_ROWS, D_CONF), lambda b, t: (b, t, 0)),
        out_shape=jax.ShapeDtypeStruct((n_seq, seq_len, D_CONF), BF16),
        scratch_shapes=[pltpu.VMEM((SUBLANES, CONF_ROWS + 2 * CONF_HALO, D_CONF), F32)],
        compiler_params=_params("arbitrary", "arbitrary"),
        name="conformer",
    )(p_cf, p_cf, p_cf, dw_w, vec, pw_w)


def _shift_sub(x, s, fill, reverse):
    r = x.shape[1]
    sub = lax.broadcasted_iota(jnp.int32, (1, r, 1), 1)
    if reverse:
        return jnp.where(sub < r - s, pltpu.roll(x, r - s, 1), fill)
    return jnp.where(sub >= s, pltpu.roll(x, s, 1), fill)


def _scan_chunk(a, b, carry, reverse):
    rows, lanes = a.shape
    n = rows // SUBLANES
    a = a.reshape(n, SUBLANES, lanes)
    b = b.reshape(n, SUBLANES, lanes)
    s = 1
    while s < SUBLANES:
        b = b + a * _shift_sub(b, s, 0.0, reverse)
        a = a * _shift_sub(a, s, 1.0, reverse)
        s *= 2
    edge = 0 if reverse else SUBLANES - 1
    at = jnp.broadcast_to(a[:, edge:edge + 1, :], a.shape)
    bt = jnp.broadcast_to(b[:, edge:edge + 1, :], b.shape)
    h_in = [None] * n
    for v in (range(n - 1, -1, -1) if reverse else range(n)):
        h_in[v] = carry
        carry = bt[v] + at[v] * carry
    h = b + a * jnp.stack(h_in)
    return h.reshape(rows, lanes), carry


def _rglru_kernel(p_ref, cw_ref, w_ref, vec_ref, o_ref, xp_ref, xr_ref, hf_ref):
    seq_len = p_ref.shape[1]
    lanes = p_ref.shape[2]
    n_chunk = seq_len // RG_CHUNK
    zeros = jnp.zeros((RG_PAD, lanes), F32)
    xp_ref[0:RG_PAD, :] = zeros
    xp_ref[RG_PAD + seq_len:, :] = zeros
    xp_ref[RG_PAD:RG_PAD + seq_len, :] = p_ref[0].astype(F32)
    cw = cw_ref[...]
    vec = vec_ref[0]
    conv_b = vec[0:1]

    def gates(xr, d):
        g = jnp.dot(xr.astype(BF16), w_ref[0, d], preferred_element_type=F32)
        r = jax.nn.sigmoid(g[:, :lanes] + vec[1 + 4 * d:2 + 4 * d])
        i = jax.nn.sigmoid(g[:, lanes:] + vec[2 + 4 * d:3 + 4 * d])
        sp = jax.nn.softplus(-vec[3 + 4 * d:4 + 4 * d])
        a = jnp.exp(-RG_C * r * sp)
        return a, jnp.sqrt(1.0 - a * a) * (i * xr)

    def fwd(c, carry):
        t0 = pl.multiple_of(c * RG_CHUNK, RG_CHUNK)
        x = xp_ref[pl.ds(t0, RG_CHUNK + 2 * RG_PAD), :]
        xr = conv_b
        for k in range(4):
            xr = xr + cw[k:k + 1] * x[RG_PAD - 2 + k:RG_PAD - 2 + k + RG_CHUNK]
        xr_ref[pl.ds(t0, RG_CHUNK), :] = xr
        a, bt = gates(xr, 0)
        h, carry = _scan_chunk(a, bt, carry, False)
        hf_ref[pl.ds(t0, RG_CHUNK), :] = h
        return carry

    lax.fori_loop(0, n_chunk, fwd, jnp.zeros((SUBLANES, lanes), F32), unroll=2)

    def bwd(i, carry):
        t0 = pl.multiple_of((n_chunk - 1 - i) * RG_CHUNK, RG_CHUNK)
        a, bt = gates(xr_ref[pl.ds(t0, RG_CHUNK), :], 1)
        h, carry = _scan_chunk(a, bt, carry, True)
        o_ref[0, pl.ds(t0, RG_CHUNK), :] = (hf_ref[pl.ds(t0, RG_CHUNK), :] + h).astype(o_ref.dtype)
        return carry

    lax.fori_loop(0, n_chunk, bwd, jnp.zeros((SUBLANES, lanes), F32), unroll=2)


def _rglru(p_rg, conv_w, w_cat, vec):
    n_seq, seq_len, _ = p_rg.shape
    n_cb = D_RG // LANES
    blk = pl.BlockSpec((1, seq_len, LANES), lambda b, c: (b, 0, c))
    return pl.pallas_call(
        _rglru_kernel,
        grid=(n_seq, n_cb),
        in_specs=[
            blk,
            pl.BlockSpec((4, LANES), lambda b, c: (0, c)),
            pl.BlockSpec((1, 2, LANES, 2 * LANES), lambda b, c: (c, 0, 0, 0)),
            pl.BlockSpec((1, 2 * SUBLANES, LANES), lambda b, c: (c, 0, 0)),
        ],
        out_specs=blk,
        out_shape=jax.ShapeDtypeStruct((n_seq, seq_len, D_RG), BF16),
        scratch_shapes=[pltpu.VMEM((seq_len + 2 * RG_PAD, LANES), F32), pltpu.VMEM((seq_len, LANES), F32),
                        pltpu.VMEM((seq_len, LANES), F32)],
        compiler_params=_params("arbitrary", "arbitrary"),
        name="rglru",
    )(p_rg, conv_w, w_cat, vec)


def _out_proj_kernel(yh_ref, ycf_ref, yrg_ref, gate_ref, *refs, n_x, n_first):
    x_refs = refs[:n_x]
    gg_ref, w_ref, fg_ref = refs[n_x:n_x + 3]
    o_refs = refs[n_x + 3:]
    b = pl.program_id(1)
    gg = gg_ref[...]
    yh = jnp.concatenate([yh_ref[k, 0].T for k in range(yh_ref.shape[0])], axis=0)
    y = jnp.concatenate([
        _rms(yh, gg[:, :D_HYENA]),
        _rms(ycf_ref[0].astype(F32), gg[:, D_HYENA:D_HYENA + D_CONF]),
        _rms(yrg_ref[0].astype(F32), gg[:, D_HYENA + D_CONF:]),
    ], axis=-1)
    gate = gate_ref[0].astype(F32)
    y = y * (gate * jax.nn.sigmoid(gate))
    out = _x_tile(x_refs, b, n_first) + jnp.dot(y.astype(BF16), w_ref[...], preferred_element_type=F32)
    if len(o_refs) == 1:
        o_refs[0][0] = out
        return
    out = _rms(out, fg_ref[...])

    @pl.when(b < n_first)
    def _():
        o_refs[0][0] = out

    @pl.when(b >= n_first)
    def _():
        o_refs[1][0] = out


def _out_proj(y_hy, y_cf, y_rg, gate, xs, grp_g, w_out, final_g, n_first, split):
    nj, n_seq, n_ch, tb = y_hy.shape
    seq_len = nj * tb
    per = OUT_ROWS // tb
    tok = lambda w: pl.BlockSpec((1, OUT_ROWS, w), lambda j, b: (b, j, 0))
    vec = lambda w: pl.BlockSpec((1, w), lambda j, b: (0, 0))
    if not split:
        out_specs = [tok(D_MODEL)]
        out_shape = [jax.ShapeDtypeStruct((n_seq, seq_len, D_MODEL), F32)]
    else:
        out_specs = [
            pl.BlockSpec((1, OUT_ROWS, D_MODEL), lambda j, b: (jnp.minimum(b, n_first - 1), j, 0)),
            pl.BlockSpec((1, OUT_ROWS, D_MODEL), lambda j, b: (jnp.maximum(b - n_first, 0), j, 0)),
        ]
        out_shape = [jax.ShapeDtypeStruct((n, seq_len, D_MODEL), F32) for n in (n_first, n_seq - n_first)]
    return pl.pallas_call(
        functools.partial(_out_proj_kernel, n_x=len(xs), n_first=n_first),
        grid=(nj // per, n_seq),
        in_specs=[
            pl.BlockSpec((per, 1, n_ch, tb), lambda j, b: (j, b, 0, 0)),
            tok(D_CONF), tok(D_RG), tok(D_MIX),
        ] + _x_specs(xs, OUT_ROWS, lambda j, b: (b, j), True) + [
            vec(D_MIX), pl.BlockSpec((D_MIX, D_MODEL), lambda j, b: (0, 0)), vec(D_MODEL),
        ],
        out_specs=out_specs,
        out_shape=out_shape,
        compiler_params=_params("arbitrary", "arbitrary"),
        name="out_proj",
    )(y_hy, y_cf, y_rg, gate, *xs, grp_g, w_out, final_g)


def _rg_gate_weights(wa, wx):
    per = LANES // GROUP_WIDTH
    n_cb = D_RG // LANES

    def diag(w):
        w = w.reshape(n_cb, per, GROUP_WIDTH, GROUP_WIDTH)
        eye = jnp.eye(per, dtype=w.dtype)
        return jnp.einsum("cpij,pq->cpiqj", w, eye).reshape(n_cb, LANES, LANES)

    dirs = [jnp.concatenate([diag(wa[d]), diag(wx[d])], axis=-1) for d in range(2)]
    return jnp.stack(dirs, axis=1).astype(BF16)


def _layer(xs, n_first, split, final_g, norm_g, w_in, hy_conv_w, hy_conv_b, hy_w1, hy_b1, hy_w2, hy_b2, hy_w3,
           hy_b3, hy_w4, hy_freq, hy_d, cf_dw_w, cf_dw_b, cf_ln_g, cf_ln_b, cf_pw_w, cf_pw_b, rg_conv_w,
           rg_conv_b, rg_wa, rg_ba, rg_wx, rg_bx, rg_lam, grp_g, w_out):
    seq_len = xs[0].shape[1]
    s1 = 3 * D_HYENA
    g_row = norm_g.reshape(1, D_MODEL)

    kfull = _filters(hy_w1, hy_b1, hy_w2, hy_b2, hy_w3, hy_b3, hy_w4, hy_freq, seq_len)
    cw = jnp.concatenate([hy_conv_w, hy_conv_b[None]], axis=0)
    cw = jnp.broadcast_to(cw[:, :, None], (4, s1, LANES))
    p_cf, p_rg, gate, xn = _in_proj(xs, g_row, w_in[:, s1:].astype(BF16))
    u, x0 = _hy_proj(xn, w_in[:, :s1].T.astype(BF16), cw)
    y_hy = _longconv(kfull, u, x0, hy_d)

    zrow = jnp.zeros((D_CONF,), F32)
    cf_vec = jnp.stack([cf_dw_b, cf_ln_g, cf_ln_b, cf_pw_b, zrow, zrow, zrow, zrow])
    y_cf = _conformer(p_cf, cf_dw_w, cf_vec, cf_pw_w.astype(BF16))

    n_cb = D_RG // LANES
    zr = jnp.zeros((D_RG,), F32)
    rg_rows = [rg_conv_b, rg_ba[0], rg_bx[0], rg_lam[0], zr, rg_ba[1], rg_bx[1], rg_lam[1]] + [zr] * 8
    rg_vec = jnp.stack(rg_rows).reshape(2 * SUBLANES, n_cb, LANES).transpose(1, 0, 2)
    y_rg = _rglru(p_rg, rg_conv_w, _rg_gate_weights(rg_wa, rg_wx), rg_vec)

    return _out_proj(y_hy, y_cf, y_rg, gate, xs, grp_g.reshape(1, D_MIX), w_out.astype(BF16),
                     final_g.reshape(1, D_MODEL), n_first, split)


def kernel(x_prompt, x_sample, norm_g, w_in, hy_conv_w, hy_conv_b, hy_w1, hy_b1, hy_w2, hy_b2, hy_w3, hy_b3, hy_w4, hy_freq, hy_d, cf_dw_w, cf_dw_b, cf_ln_g, cf_ln_b, cf_pw_w, cf_pw_b, rg_conv_w, rg_conv_b, rg_wa, rg_ba, rg_wx, rg_bx, rg_lam, grp_g, w_out, final_g):
    layer_params = (norm_g, w_in, hy_conv_w, hy_conv_b, hy_w1, hy_b1, hy_w2, hy_b2, hy_w3, hy_b3, hy_w4, hy_freq,
                    hy_d, cf_dw_w, cf_dw_b, cf_ln_g, cf_ln_b, cf_pw_w, cf_pw_b, rg_conv_w, rg_conv_b, rg_wa,
                    rg_ba, rg_wx, rg_bx, rg_lam, grp_g, w_out)
    assert x_prompt.shape[1:] == x_sample.shape[1:]
    n_prompt = x_prompt.shape[0]
    n_rows = (n_prompt + x_sample.shape[0]) * (x_prompt.shape[1] // TOEPLITZ_BLOCK)
    assert n_rows % ROW_TILE == 0, "longconv needs (sequences x time blocks) to fill whole bf16 row tiles"
    depth = norm_g.shape[0]
    xs = (x_prompt, x_sample)
    for l in range(depth):
        xs = tuple(_layer(xs, n_prompt, l == depth - 1, final_g, *[w[l] for w in layer_params]))
    return xs
```

```python
import functools
import math

import jax
import jax.numpy as jnp
from jax import lax
from jax.experimental import pallas as pl
from jax.experimental.pallas import tpu as pltpu

F32 = jnp.float32
BF16 = jnp.bfloat16

D_MODEL = 1024
D_HYENA = 384
D_CONF = 256
D_RG = 384
D_MIX = D_HYENA + D_CONF + D_RG
GROUP_WIDTH = 64
HYENA_BANDS = 16
HYENA_ORDER = 64
HYENA_TARGET = 1e-2
HYENA_FAST_DECAY_PCT = 0.3
HYENA_SLOW_DECAY_PCT = 1.5
CONF_K = 31
RG_C = 8.0
EPS = 1e-6

LANES = 128
SUBLANES = 8
MXU_DIM = 256
VMEM_LIMIT_BYTES = 56 * 1024 * 1024

TOEPLITZ_BLOCK = MXU_DIM
ROW_TILE = 16
FILTER_TILE = 2048
PROJ_ROWS = 512
CONF_HALO = 16
CONF_SUB = 64
OUT_ROWS = 512
HY_ROWS = 128
RG_CHUNK = 256
RG_PAD = 8


def _params(*sem):
    return pltpu.CompilerParams(dimension_semantics=sem, vmem_limit_bytes=VMEM_LIMIT_BYTES)


def _rms(x, g):
    return x * lax.rsqrt(jnp.mean(x * x, axis=-1, keepdims=True) + EPS) * g


def _x_tile(x_refs, b, n_first):
    if len(x_refs) == 1:
        return x_refs[0][0]
    return jnp.where(b < n_first, x_refs[0][0], x_refs[1][0])


def _x_specs(xs, rows, seq_tile, hold_tile):
    if len(xs) == 1:
        return [pl.BlockSpec((1, rows, D_MODEL), lambda *g: (*seq_tile(*g), 0))]
    n1 = xs[0].shape[0]
    last = xs[0].shape[1] // rows - 1

    def first(*g):
        b, t = seq_tile(*g)
        return (jnp.minimum(b, n1 - 1), t if hold_tile else jnp.where(b < n1, t, last), 0)

    def second(*g):
        b, t = seq_tile(*g)
        return (jnp.maximum(b - n1, 0), t if hold_tile else jnp.where(b < n1, 0, t), 0)

    return [pl.BlockSpec((1, rows, D_MODEL), first), pl.BlockSpec((1, rows, D_MODEL), second)]


def _filters_kernel(w1t_ref, w1c_ref, w1s_ref, b1_ref, w2_ref, b2_ref, w3_ref, b3_ref, w4_ref,
                    fr_ref, o_ref, *, seq_len):
    i = pl.program_id(0)
    tile = o_ref.shape[1]
    m = i * tile + lax.broadcasted_iota(jnp.int32, (1, tile), 1)
    pos = jnp.abs(m - seq_len).astype(F32)
    t = pos / (seq_len - 1.0)
    w = (2.0 * math.pi / seq_len) * pos
    band = lax.broadcasted_iota(jnp.int32, (HYENA_BANDS, 1), 0).astype(F32)
    f = 1e-4 + band * ((HYENA_BANDS - 1 - 1e-4) / (HYENA_BANDS - 1))
    fw = f * w
    hi = lax.Precision.HIGHEST
    fr = fr_ref[...]
    pre = (w1t_ref[...] * t + jnp.dot(w1c_ref[...], jnp.cos(fw), precision=hi)
           - jnp.dot(w1s_ref[...], jnp.sin(fw), precision=hi))
    h = jnp.sin(fr * (pre + b1_ref[...]))
    h = jnp.sin(fr * (jnp.dot(w2_ref[...], h, precision=hi) + b2_ref[...]))
    h = jnp.sin(fr * (jnp.dot(w3_ref[...], h, precision=hi) + b3_ref[...]))
    k = jnp.dot(w4_ref[...], h, precision=hi)
    max_decay = math.log(HYENA_TARGET) / HYENA_FAST_DECAY_PCT
    min_decay = math.log(HYENA_TARGET) / HYENA_SLOW_DECAY_PCT
    ch = lax.broadcasted_iota(jnp.int32, (D_HYENA, 1), 0).astype(F32)
    delta = jnp.abs(min_decay + ch * ((max_decay - min_decay) / (D_HYENA - 1)))
    decay = jnp.exp(-t * delta)
    k = jnp.where(m >= seq_len, k[:D_HYENA], k[D_HYENA:]) * decay
    o_ref[...] = jnp.where(m == 0, 0.0, k)


def _filters(w1, b1, w2, b2, w3, b3, w4, freq, seq_len):
    w1t = w1.T
    col = lambda v: v.reshape(-1, 1)
    args = (w1t[:, 0:1], w1t[:, 1:1 + HYENA_BANDS], w1t[:, 1 + HYENA_BANDS:], col(b1), w2.T, col(b2),
            w3.T, col(b3), w4.T, col(freq))
    full = lambda a: pl.BlockSpec(a.shape, lambda i: (0, 0))
    return pl.pallas_call(
        functools.partial(_filters_kernel, seq_len=seq_len),
        grid=(2 * seq_len // FILTER_TILE,),
        in_specs=[full(a) for a in args],
        out_specs=pl.BlockSpec((D_HYENA, FILTER_TILE), lambda i: (0, i)),
        out_shape=jax.ShapeDtypeStruct((D_HYENA, 2 * seq_len), F32),
        compiler_params=_params("arbitrary"),
        name="filters",
    )(*args)


def _hy_proj_kernel(xn_ref, w_ref, cw_ref, u_ref, x0_ref, p_ref, l_ref, s_ref):
    j = pl.program_id(1)
    nj = pl.num_programs(1) - 1
    half = LANES

    @pl.when(j == 0)
    def _():
        p_ref[...] = jnp.zeros(p_ref.shape, F32)
        l_ref[...] = jnp.zeros(l_ref.shape, F32)
        s_ref[...] = jnp.zeros(s_ref.shape, F32)

    p_cur = jnp.dot(w_ref[...], xn_ref[0], preferred_element_type=F32)
    s_cur = pltpu.roll(p_cur[:, :half], half - 1, 1)

    lane = lax.broadcasted_iota(jnp.int32, (1, half), 1)
    first = lane == 0
    last = lane == half - 1
    s_right = jnp.where(j < nj, s_cur, 0.0)

    def conv(rows):
        p_lo = p_ref[rows, :half]
        p_hi = p_ref[rows, half:]
        r_lo = pltpu.roll(p_lo, 1, 1)
        r_hi = pltpu.roll(p_hi, 1, 1)
        s_hi = pltpu.roll(p_hi, half - 1, 1)
        w0, w1, w2, bias = (cw_ref[k, rows, :] for k in range(4))
        lo = w0 * jnp.where(first, l_ref[rows, :], r_lo) + w1 * p_lo + w2 * jnp.where(last, s_hi, s_ref[rows, :]) + bias
        hi = w0 * jnp.where(first, r_lo, r_hi) + w1 * p_hi + w2 * jnp.where(last, s_right[rows], s_hi) + bias
        l_ref[rows, :] = r_hi
        return lo, hi

    for r in range(0, D_HYENA, HY_ROWS):
        x0 = conv(slice(r, r + HY_ROWS))
        x1 = conv(slice(D_HYENA + r, D_HYENA + r + HY_ROWS))
        v = conv(slice(2 * D_HYENA + r, 2 * D_HYENA + r + HY_ROWS))
        for h in range(2):
            cols = slice(h * half, (h + 1) * half)
            x0_ref[0, 0, r:r + HY_ROWS, cols] = x0[h]
            u_ref[0, 0, r:r + HY_ROWS, cols] = v[h] * x1[h]
    p_ref[...] = p_cur
    s_ref[...] = s_cur


def _hy_proj(xn, w_hy_t, cw):
    n_seq, _, seq_len = xn.shape
    tb = TOEPLITZ_BLOCK
    nj = seq_len // tb
    out = jax.ShapeDtypeStruct((nj, n_seq, D_HYENA, tb), F32)
    out_spec = pl.BlockSpec((1, 1, D_HYENA, tb), lambda b, j: (jnp.maximum(j - 1, 0), b, 0, 0))
    return pl.pallas_call(
        _hy_proj_kernel,
        grid=(n_seq, nj + 1),
        in_specs=[
            pl.BlockSpec((1, D_MODEL, tb), lambda b, j: (b, 0, jnp.minimum(j, nj - 1))),
            pl.BlockSpec((3 * D_HYENA, D_MODEL), lambda b, j: (0, 0)),
            pl.BlockSpec((4, 3 * D_HYENA, LANES), lambda b, j: (0, 0, 0)),
        ],
        out_specs=[out_spec, out_spec],
        out_shape=[out, out],
        scratch_shapes=[pltpu.VMEM((3 * D_HYENA, tb), F32), pltpu.VMEM((3 * D_HYENA, LANES), F32),
                        pltpu.VMEM((3 * D_HYENA, LANES), F32)],
        compiler_params=_params("arbitrary", "arbitrary"),
        name="hy_proj",
    )(xn, w_hy_t, cw)


def _longconv_kernel(k_ref, u_ref, x0_ref, d_ref, o_ref, ut_ref, yt_ref, s_ref, ub_ref, y_ref, *, n_seq):
    tb = TOEPLITZ_BLOCK
    half = tb // 2
    rows, n_ch, _ = u_ref.shape
    nj = rows // n_seq
    seq_len = nj * tb
    step = math.gcd(n_seq, ROW_TILE)
    n_phase = ROW_TILE // step

    tri = (lax.broadcasted_iota(jnp.int32, (half, half), 1) >= lax.broadcasted_iota(jnp.int32, (half, half), 0))
    s_ref[rows:, :] = jnp.zeros((2 * ROW_TILE, tb), F32)
    ut_ref[...] = pltpu.einshape("rcl->crl", u_ref[...])

    def channel(i, carry):
        def circ(q):
            seg = jnp.broadcast_to(k_ref[i, :, half * q:half * (q + 1)], (half, half))
            return pltpu.roll(seg, 0, 1, stride=1, stride_axis=0)

        s_ref[0:rows, :] = ut_ref[i]
        for p in range(n_phase):
            ub_ref[p] = s_ref[p * step:p * step + rows + ROW_TILE, :].astype(BF16)
        y_ref[...] = jnp.zeros(y_ref.shape, F32)

        q = (seq_len - tb * (nj - 1)) // half
        c_prev = circ(q - 1)
        g_prev = jnp.where(tri, c_prev, circ(q - 2)).astype(BF16)
        for d in range(-(nj - 1), nj):
            c_a = circ(q)
            c_b = circ(q + 1)
            g_a = jnp.where(tri, c_a, c_prev).astype(BF16)
            g_b = jnp.where(tri, c_b, c_a).astype(BF16)
            tile = jnp.concatenate(
                [jnp.concatenate([g_a, g_b], axis=1), jnp.concatenate([g_prev, g_a], axis=1)], axis=0)
            c_prev, g_prev = c_b, g_b
            q += 2
            n = -(-n_seq * (nj - abs(d)) // ROW_TILE) * ROW_TILE
            src = 0 if d >= 0 else n_seq * -d
            dst = n_seq * d if d >= 0 else 0
            ps, pd = (src % ROW_TILE) // step, (dst % ROW_TILE) // step
            src -= ps * step
            dst += ROW_TILE - pd * step
            y_ref[pd, dst:dst + n, :] += jnp.dot(ub_ref[ps, src:src + n, :], tile, preferred_element_type=F32)

        y = y_ref[0, ROW_TILE:ROW_TILE + rows, :]
        for p in range(1, n_phase):
            y = y + y_ref[p, ROW_TILE - p * step:ROW_TILE - p * step + rows, :]
        yt_ref[i] = y
        return carry

    lax.fori_loop(0, n_ch, channel, 0, unroll=2)
    y = pltpu.einshape("crl->rcl", yt_ref[...])
    o_ref[...] = (y + u_ref[...] * d_ref[...][None]) * x0_ref[...]


def _longconv(kfull, u, x0, d_skip):
    nj, n_seq, n_ch, tb = u.shape
    rows = nj * n_seq
    seq_len = nj * tb
    n_phase = ROW_TILE // math.gcd(n_seq, ROW_TILE)
    blk = pl.BlockSpec((rows, SUBLANES, tb), lambda c: (0, c, 0))
    out = pl.pallas_call(
        functools.partial(_longconv_kernel, n_seq=n_seq),
        grid=(n_ch // SUBLANES,),
        in_specs=[
            pl.BlockSpec((SUBLANES, 1, 2 * seq_len), lambda c: (c, 0, 0)),
            blk, blk,
            pl.BlockSpec((SUBLANES, tb), lambda c: (c, 0)),
        ],
        out_specs=blk,
        out_shape=jax.ShapeDtypeStruct((rows, n_ch, tb), F32),
        scratch_shapes=[pltpu.VMEM((SUBLANES, rows, tb), F32), pltpu.VMEM((SUBLANES, rows, tb), F32),
                        pltpu.VMEM((rows + 2 * ROW_TILE, tb), F32), pltpu.VMEM((n_phase, rows + ROW_TILE, tb), BF16),
                        pltpu.VMEM((n_phase, rows + 2 * ROW_TILE, tb), F32)],
        compiler_params=_params("arbitrary"),
        name="longconv",
    )(kfull.reshape(n_ch, 1, 2 * seq_len), u.reshape(rows, n_ch, tb), x0.reshape(rows, n_ch, tb),
      jnp.broadcast_to(d_skip[:, None], (n_ch, tb)))
    return out.reshape(nj, n_seq, n_ch, tb)


def _proj_conf_kernel(*refs, n_x, n_first, nt):
    x_refs = refs[:n_x]
    (g_ref, w_ref, dw_ref, vec_ref, pw_ref, rg_ref, gate_ref, xn_ref, ycf_ref,
     stage_ref, gp_ref, ext_ref, yb_ref) = refs[n_x:]
    s = pl.program_id(0)
    rows = PROJ_ROWS
    half = (CONF_K - 1) // 2
    n_ext = rows + 2 * CONF_HALO

    @pl.when(s == 0)
    def _():
        stage_ref[...] = jnp.zeros(stage_ref.shape, F32)
        gp_ref[...] = jnp.zeros(gp_ref.shape, F32)

    b = jnp.minimum(s, pl.num_programs(0) - 3) // nt
    xn = _rms(_x_tile(x_refs, b, n_first), g_ref[...])
    xn_ref[0] = xn.T.astype(BF16)
    p = jnp.dot(xn.astype(BF16), w_ref[...], preferred_element_type=F32)
    rg_ref[0] = p[:, 2 * D_CONF:2 * D_CONF + D_RG].astype(BF16)
    gate_ref[0] = p[:, 2 * D_CONF + D_RG:].astype(BF16)
    g_cur = p[:, :D_CONF] * jax.nn.sigmoid(p[:, D_CONF:2 * D_CONF])

    for r in range(1, SUBLANES):
        ext_ref[r, 0:n_ext - SUBLANES, :] = stage_ref[r:r + n_ext - SUBLANES, :]
    vec = vec_ref[...]
    dw_b, ln_g, ln_b, pw_b = vec[0:1], vec[1:2], vec[2:3], vec[3:4]
    dw = dw_ref[...]
    for c in range(rows // CONF_SUB):
        acc = jnp.broadcast_to(dw_b, (CONF_SUB, D_CONF))
        for k in range(CONF_K):
            off = CONF_HALO - half + k
            start = c * CONF_SUB + off - off % SUBLANES
            if off % SUBLANES == 0:
                tap = stage_ref[start:start + CONF_SUB, :]
            else:
                tap = ext_ref[off % SUBLANES, start:start + CONF_SUB, :]
            acc = acc + dw[k:k + 1] * tap
        mu = jnp.mean(acc, axis=-1, keepdims=True)
        xc = acc - mu
        var = jnp.mean(xc * xc, axis=-1, keepdims=True)
        y = xc * lax.rsqrt(var + EPS) * ln_g + ln_b
        yb_ref[c * CONF_SUB:(c + 1) * CONF_SUB, :] = (y * jax.nn.sigmoid(y)).astype(BF16)
    ycf_ref[0] = (jnp.dot(yb_ref[...], pw_ref[...], preferred_element_type=F32) + pw_b).astype(ycf_ref.dtype)

    t_next = (s + nt - 1) % nt
    tail = stage_ref[rows:rows + CONF_HALO, :]
    stage_ref[0:CONF_HALO, :] = jnp.where(t_next > 0, tail, 0.0)
    stage_ref[CONF_HALO:CONF_HALO + rows, :] = gp_ref[...]
    stage_ref[CONF_HALO + rows:, :] = jnp.where(t_next < nt - 1, g_cur[0:CONF_HALO], 0.0)
    gp_ref[...] = g_cur


def _proj_conf(xs, norm_g, w_rest, dw_w, vec, pw_w):
    n_seq = sum(x.shape[0] for x in xs)
    seq_len = xs[0].shape[1]
    n_col = w_rest.shape[1]
    nt = seq_len // PROJ_ROWS
    n_tiles = n_seq * nt
    cur = lambda s: divmod(jnp.minimum(s, n_tiles - 1), nt)
    old = lambda s: divmod(jnp.maximum(s - 2, 0), nt)
    row = lambda w, at: pl.BlockSpec((1, PROJ_ROWS, w), lambda s: (*at(s), 0))
    full = lambda a: pl.BlockSpec(a.shape, lambda s: (0, 0))
    return pl.pallas_call(
        functools.partial(_proj_conf_kernel, n_x=len(xs), n_first=xs[0].shape[0], nt=nt),
        grid=(n_tiles + 2,),
        in_specs=_x_specs(xs, PROJ_ROWS, cur, False) + [full(a) for a in (norm_g, w_rest, dw_w, vec, pw_w)],
        out_specs=[row(D_RG, cur), row(D_MIX, cur),
                   pl.BlockSpec((1, D_MODEL, PROJ_ROWS), lambda s: (cur(s)[0], 0, cur(s)[1])),
                   row(D_CONF, old)],
        out_shape=[jax.ShapeDtypeStruct((n_seq, seq_len, D_RG), BF16),
                   jax.ShapeDtypeStruct((n_seq, seq_len, D_MIX), BF16),
                   jax.ShapeDtypeStruct((n_seq, D_MODEL, seq_len), BF16),
                   jax.ShapeDtypeStruct((n_seq, seq_len, D_CONF), BF16)],
        scratch_shapes=[pltpu.VMEM((PROJ_ROWS + 2 * CONF_HALO, D_CONF), F32),
                        pltpu.VMEM((PROJ_ROWS, D_CONF), F32),
                        pltpu.VMEM((SUBLANES, PROJ_ROWS + 2 * CONF_HALO, D_CONF), F32),
                        pltpu.VMEM((PROJ_ROWS, D_CONF), BF16)],
        compiler_params=_params("arbitrary"),
        name="proj_conf",
    )(*xs, norm_g, w_rest, dw_w, vec, pw_w)


def _shift_sub(x, s, fill, reverse):
    r = x.shape[1]
    sub = lax.broadcasted_iota(jnp.int32, (1, r, 1), 1)
    if reverse:
        return jnp.where(sub < r - s, pltpu.roll(x, r - s, 1), fill)
    return jnp.where(sub >= s, pltpu.roll(x, s, 1), fill)


def _scan_chunk(a, b, carry, reverse):
    rows, lanes = a.shape
    n = rows // SUBLANES
    a = a.reshape(n, SUBLANES, lanes)
    b = b.reshape(n, SUBLANES, lanes)
    s = 1
    while s < SUBLANES:
        b = b + a * _shift_sub(b, s, 0.0, reverse)
        a = a * _shift_sub(a, s, 1.0, reverse)
        s *= 2
    edge = 0 if reverse else SUBLANES - 1
    at = jnp.broadcast_to(a[:, edge:edge + 1, :], a.shape)
    bt = jnp.broadcast_to(b[:, edge:edge + 1, :], b.shape)
    h_in = [None] * n
    for v in (range(n - 1, -1, -1) if reverse else range(n)):
        h_in[v] = carry
        carry = bt[v] + at[v] * carry
    h = b + a * jnp.stack(h_in)
    return h.reshape(rows, lanes), carry


def _rglru_kernel(p_ref, cw_ref, w_ref, vec_ref, o_ref, xp_ref, xr_ref, hf_ref):
    seq_len = p_ref.shape[1]
    lanes = p_ref.shape[2]
    n_chunk = seq_len // RG_CHUNK
    zeros = jnp.zeros((RG_PAD, lanes), F32)
    xp_ref[0:RG_PAD, :] = zeros
    xp_ref[RG_PAD + seq_len:, :] = zeros
    xp_ref[RG_PAD:RG_PAD + seq_len, :] = p_ref[0].astype(F32)
    cw = cw_ref[...]
    vec = vec_ref[0]
    conv_b = vec[0:1]

    def gates(xr, d):
        g = jnp.dot(xr.astype(BF16), w_ref[0, d], preferred_element_type=F32)
        r = jax.nn.sigmoid(g[:, :lanes] + vec[1 + 4 * d:2 + 4 * d])
        i = jax.nn.sigmoid(g[:, lanes:] + vec[2 + 4 * d:3 + 4 * d])
        sp = jax.nn.softplus(-vec[3 + 4 * d:4 + 4 * d])
        a = jnp.exp(-RG_C * r * sp)
        return a, jnp.sqrt(1.0 - a * a) * (i * xr)

    def fwd(c, carry):
        t0 = pl.multiple_of(c * RG_CHUNK, RG_CHUNK)
        x = xp_ref[pl.ds(t0, RG_CHUNK + 2 * RG_PAD), :]
        xr = conv_b
        for k in range(4):
            xr = xr + cw[k:k + 1] * x[RG_PAD - 2 + k:RG_PAD - 2 + k + RG_CHUNK]
        xr_ref[pl.ds(t0, RG_CHUNK), :] = xr
        a, bt = gates(xr, 0)
        h, carry = _scan_chunk(a, bt, carry, False)
        hf_ref[pl.ds(t0, RG_CHUNK), :] = h
        return carry

    lax.fori_loop(0, n_chunk, fwd, jnp.zeros((SUBLANES, lanes), F32), unroll=2)

    def bwd(i, carry):
        t0 = pl.multiple_of((n_chunk - 1 - i) * RG_CHUNK, RG_CHUNK)
        a, bt = gates(xr_ref[pl.ds(t0, RG_CHUNK), :], 1)
        h, carry = _scan_chunk(a, bt, carry, True)
        o_ref[0, pl.ds(t0, RG_CHUNK), :] = (hf_ref[pl.ds(t0, RG_CHUNK), :] + h).astype(o_ref.dtype)
        return carry

    lax.fori_loop(0, n_chunk, bwd, jnp.zeros((SUBLANES, lanes), F32), unroll=2)


def _rglru(p_rg, conv_w, w_cat, vec):
    n_seq, seq_len, _ = p_rg.shape
    n_cb = D_RG // LANES
    blk = pl.BlockSpec((1, seq_len, LANES), lambda b, c: (b, 0, c))
    return pl.pallas_call(
        _rglru_kernel,
        grid=(n_seq, n_cb),
        in_specs=[
            blk,
            pl.BlockSpec((4, LANES), lambda b, c: (0, c)),
            pl.BlockSpec((1, 2, LANES, 2 * LANES), lambda b, c: (c, 0, 0, 0)),
            pl.BlockSpec((1, 2 * SUBLANES, LANES), lambda b, c: (c, 0, 0)),
        ],
        out_specs=blk,
        out_shape=jax.ShapeDtypeStruct((n_seq, seq_len, D_RG), BF16),
        scratch_shapes=[pltpu.VMEM((seq_len + 2 * RG_PAD, LANES), F32), pltpu.VMEM((seq_len, LANES), F32),
                        pltpu.VMEM((seq_len, LANES), F32)],
        compiler_params=_params("arbitrary", "arbitrary"),
        name="rglru",
    )(p_rg, conv_w, w_cat, vec)


def _out_proj_kernel(yh_ref, ycf_ref, yrg_ref, gate_ref, *refs, n_x, n_first):
    x_refs = refs[:n_x]
    gg_ref, w_ref, fg_ref = refs[n_x:n_x + 3]
    o_refs = refs[n_x + 3:]
    b = pl.program_id(1)
    gg = gg_ref[...]
    yh = jnp.concatenate([yh_ref[k, 0].T for k in range(yh_ref.shape[0])], axis=0)
    y = jnp.concatenate([
        _rms(yh, gg[:, :D_HYENA]),
        _rms(ycf_ref[0].astype(F32), gg[:, D_HYENA:D_HYENA + D_CONF]),
        _rms(yrg_ref[0].astype(F32), gg[:, D_HYENA + D_CONF:]),
    ], axis=-1)
    gate = gate_ref[0].astype(F32)
    y = y * (gate * jax.nn.sigmoid(gate))
    out = _x_tile(x_refs, b, n_first) + jnp.dot(y.astype(BF16), w_ref[...], preferred_element_type=F32)
    if len(o_refs) == 1:
        o_refs[0][0] = out
        return
    out = _rms(out, fg_ref[...])

    @pl.when(b < n_first)
    def _():
        o_refs[0][0] = out

    @pl.when(b >= n_first)
    def _():
        o_refs[1][0] = out


def _out_proj(y_hy, y_cf, y_rg, gate, xs, grp_g, w_out, final_g, n_first, split):
    nj, n_seq, n_ch, tb = y_hy.shape
    seq_len = nj * tb
    per = OUT_ROWS // tb
    tok = lambda w: pl.BlockSpec((1, OUT_ROWS, w), lambda j, b: (b, j, 0))
    vec = lambda w: pl.BlockSpec((1, w), lambda j, b: (0, 0))
    if not split:
        out_specs = [tok(D_MODEL)]
        out_shape = [jax.ShapeDtypeStruct((n_seq, seq_len, D_MODEL), F32)]
    else:
        out_specs = [
            pl.BlockSpec((1, OUT_ROWS, D_MODEL), lambda j, b: (jnp.minimum(b, n_first - 1), j, 0)),
            pl.BlockSpec((1, OUT_ROWS, D_MODEL), lambda j, b: (jnp.maximum(b - n_first, 0), j, 0)),
        ]
        out_shape = [jax.ShapeDtypeStruct((n, seq_len, D_MODEL), F32) for n in (n_first, n_seq - n_first)]
    return pl.pallas_call(
        functools.partial(_out_proj_kernel, n_x=len(xs), n_first=n_first),
        grid=(nj // per, n_seq),
        in_specs=[
            pl.BlockSpec((per, 1, n_ch, tb), lambda j, b: (j, b, 0, 0)),
            tok(D_CONF), tok(D_RG), tok(D_MIX),
        ] + _x_specs(xs, OUT_ROWS, lambda j, b: (b, j), True) + [
            vec(D_MIX), pl.BlockSpec((D_MIX, D_MODEL), lambda j, b: (0, 0)), vec(D_MODEL),
        ],
        out_specs=out_specs,
        out_shape=out_shape,
        compiler_params=_params("arbitrary", "arbitrary"),
        name="out_proj",
    )(y_hy, y_cf, y_rg, gate, *xs, grp_g, w_out, final_g)


def _rg_gate_weights(wa, wx):
    per = LANES // GROUP_WIDTH
    n_cb = D_RG // LANES

    def diag(w):
        w = w.reshape(n_cb, per, GROUP_WIDTH, GROUP_WIDTH)
        eye = jnp.eye(per, dtype=w.dtype)
        return jnp.einsum("cpij,pq->cpiqj", w, eye).reshape(n_cb, LANES, LANES)

    dirs = [jnp.concatenate([diag(wa[d]), diag(wx[d])], axis=-1) for d in range(2)]
    return jnp.stack(dirs, axis=1).astype(BF16)


def _layer(xs, n_first, split, final_g, norm_g, w_in, hy_conv_w, hy_conv_b, hy_w1, hy_b1, hy_w2, hy_b2, hy_w3,
           hy_b3, hy_w4, hy_freq, hy_d, cf_dw_w, cf_dw_b, cf_ln_g, cf_ln_b, cf_pw_w, cf_pw_b, rg_conv_w,
           rg_conv_b, rg_wa, rg_ba, rg_wx, rg_bx, rg_lam, grp_g, w_out):
    seq_len = xs[0].shape[1]
    s1 = 3 * D_HYENA
    g_row = norm_g.reshape(1, D_MODEL)

    kfull = _filters(hy_w1, hy_b1, hy_w2, hy_b2, hy_w3, hy_b3, hy_w4, hy_freq, seq_len)
    cw = jnp.concatenate([hy_conv_w, hy_conv_b[None]], axis=0)
    cw = jnp.broadcast_to(cw[:, :, None], (4, s1, LANES))
    zrow = jnp.zeros((D_CONF,), F32)
    cf_vec = jnp.stack([cf_dw_b, cf_ln_g, cf_ln_b, cf_pw_b, zrow, zrow, zrow, zrow])
    p_rg, gate, xn, y_cf = _proj_conf(xs, g_row, w_in[:, s1:].astype(BF16), cf_dw_w, cf_vec,
                                      cf_pw_w.astype(BF16))
    u, x0 = _hy_proj(xn, w_in[:, :s1].T.astype(BF16), cw)
    y_hy = _longconv(kfull, u, x0, hy_d)

    n_cb = D_RG // LANES
    zr = jnp.zeros((D_RG,), F32)
    rg_rows = [rg_conv_b, rg_ba[0], rg_bx[0], rg_lam[0], zr, rg_ba[1], rg_bx[1], rg_lam[1]] + [zr] * 8
    rg_vec = jnp.stack(rg_rows).reshape(2 * SUBLANES, n_cb, LANES).transpose(1, 0, 2)
    y_rg = _rglru(p_rg, rg_conv_w, _rg_gate_weights(rg_wa, rg_wx), rg_vec)

    return _out_proj(y_hy, y_cf, y_rg, gate, xs, grp_g.reshape(1, D_MIX), w_out.astype(BF16),
                     final_g.reshape(1, D_MODEL), n_first, split)


def kernel(x_prompt, x_sample, norm_g, w_in, hy_conv_w, hy_conv_b, hy_w1, hy_b1, hy_w2, hy_b2, hy_w3, hy_b3, hy_w4, hy_freq, hy_d, cf_dw_w, cf_dw_b, cf_ln_g, cf_ln_b, cf_pw_w, cf_pw_b, rg_conv_w, rg_conv_b, rg_wa, rg_ba, rg_wx, rg_bx, rg_lam, grp_g, w_out, final_g):
    layer_params = (norm_g, w_in, hy_conv_w, hy_conv_b, hy_w1, hy_b1, hy_w2, hy_b2, hy_w3, hy_b3, hy_w4, hy_freq,
                    hy_d, cf_dw_w, cf_dw_b, cf_ln_g, cf_ln_b, cf_pw_w, cf_pw_b, rg_conv_w, rg_conv_b, rg_wa,
                    rg_ba, rg_wx, rg_bx, rg_lam, grp_g, w_out)
    assert x_prompt.shape[1:] == x_sample.shape[1:]
    n_prompt = x_prompt.shape[0]
    n_rows = (n_prompt + x_sample.shape[0]) * (x_prompt.shape[1] // TOEPLITZ_BLOCK)
    assert n_rows % ROW_TILE == 0, "longconv needs (sequences x time blocks) to fill whole bf16 row tiles"
    depth = norm_g.shape[0]
    xs = (x_prompt, x_sample)
    for l in range(depth):
        xs = tuple(_layer(xs, n_prompt, l == depth - 1, final_g, *[w[l] for w in layer_params]))
    return xs
```

```python
import functools
import math

import jax
import jax.numpy as jnp
from jax import lax
from jax.experimental import pallas as pl
from jax.experimental.pallas import tpu as pltpu

F32 = jnp.float32
BF16 = jnp.bfloat16

D_MODEL = 1024
D_HYENA = 384
D_CONF = 256
D_RG = 384
D_MIX = D_HYENA + D_CONF + D_RG
GROUP_WIDTH = 64
HYENA_BANDS = 16
HYENA_ORDER = 64
HYENA_TARGET = 1e-2
HYENA_FAST_DECAY_PCT = 0.3
HYENA_SLOW_DECAY_PCT = 1.5
CONF_K = 31
RG_C = 8.0
EPS = 1e-6

LANES = 128
SUBLANES = 8
MXU_DIM = 256
VMEM_LIMIT_BYTES = 56 * 1024 * 1024

TOEPLITZ_BLOCK = MXU_DIM
ROW_TILE = 16
FILTER_TILE = 2048
PROJ_ROWS = 512
CONF_HALO = 16
CONF_SUB = 64
OUT_ROWS = 512
HY_ROWS = 128
RG_CHUNK = 256
RG_PAD = 8


def _params(*sem):
    return pltpu.CompilerParams(dimension_semantics=sem, vmem_limit_bytes=VMEM_LIMIT_BYTES)


def _rms(x, g):
    return x * lax.rsqrt(jnp.mean(x * x, axis=-1, keepdims=True) + EPS) * g


def _sigmoid(x):
    return 0.5 * jnp.tanh(0.5 * x) + 0.5


def _x_tile(x_refs, b, n_first):
    if len(x_refs) == 1:
        return x_refs[0][0]
    return jnp.where(b < n_first, x_refs[0][0], x_refs[1][0])


def _x_specs(xs, rows, seq_tile, hold_tile):
    if len(xs) == 1:
        return [pl.BlockSpec((1, rows, D_MODEL), lambda *g: (*seq_tile(*g), 0))]
    n1 = xs[0].shape[0]
    last = xs[0].shape[1] // rows - 1

    def first(*g):
        b, t = seq_tile(*g)
        return (jnp.minimum(b, n1 - 1), t if hold_tile else jnp.where(b < n1, t, last), 0)

    def second(*g):
        b, t = seq_tile(*g)
        return (jnp.maximum(b - n1, 0), t if hold_tile else jnp.where(b < n1, 0, t), 0)

    return [pl.BlockSpec((1, rows, D_MODEL), first), pl.BlockSpec((1, rows, D_MODEL), second)]


def _filters_kernel(w1t_ref, w1c_ref, w1s_ref, b1_ref, w2_ref, b2_ref, w3_ref, b3_ref, w4_ref,
                    fr_ref, o_ref, *, seq_len):
    i = pl.program_id(0)
    tile = o_ref.shape[1]
    m = i * tile + lax.broadcasted_iota(jnp.int32, (1, tile), 1)
    pos = jnp.abs(m - seq_len).astype(F32)
    t = pos / (seq_len - 1.0)
    w = (2.0 * math.pi / seq_len) * pos
    band = lax.broadcasted_iota(jnp.int32, (HYENA_BANDS, 1), 0).astype(F32)
    f = 1e-4 + band * ((HYENA_BANDS - 1 - 1e-4) / (HYENA_BANDS - 1))
    fw = f * w
    hi = lax.Precision.HIGHEST
    fr = fr_ref[...]
    pre = (w1t_ref[...] * t + jnp.dot(w1c_ref[...], jnp.cos(fw), precision=hi)
           - jnp.dot(w1s_ref[...], jnp.sin(fw), precision=hi))
    h = jnp.sin(fr * (pre + b1_ref[...]))
    h = jnp.sin(fr * (jnp.dot(w2_ref[...], h, precision=hi) + b2_ref[...]))
    h = jnp.sin(fr * (jnp.dot(w3_ref[...], h, precision=hi) + b3_ref[...]))
    k = jnp.dot(w4_ref[...], h, precision=hi)
    max_decay = math.log(HYENA_TARGET) / HYENA_FAST_DECAY_PCT
    min_decay = math.log(HYENA_TARGET) / HYENA_SLOW_DECAY_PCT
    ch = lax.broadcasted_iota(jnp.int32, (D_HYENA, 1), 0).astype(F32)
    delta = jnp.abs(min_decay + ch * ((max_decay - min_decay) / (D_HYENA - 1)))
    decay = jnp.exp(-t * delta)
    k = jnp.where(m >= seq_len, k[:D_HYENA], k[D_HYENA:]) * decay
    o_ref[...] = jnp.where(m == 0, 0.0, k)


def _filters(w1, b1, w2, b2, w3, b3, w4, freq, seq_len):
    w1t = w1.T
    col = lambda v: v.reshape(-1, 1)
    args = (w1t[:, 0:1], w1t[:, 1:1 + HYENA_BANDS], w1t[:, 1 + HYENA_BANDS:], col(b1), w2.T, col(b2),
            w3.T, col(b3), w4.T, col(freq))
    full = lambda a: pl.BlockSpec(a.shape, lambda i: (0, 0))
    return pl.pallas_call(
        functools.partial(_filters_kernel, seq_len=seq_len),
        grid=(2 * seq_len // FILTER_TILE,),
        in_specs=[full(a) for a in args],
        out_specs=pl.BlockSpec((D_HYENA, FILTER_TILE), lambda i: (0, i)),
        out_shape=jax.ShapeDtypeStruct((D_HYENA, 2 * seq_len), F32),
        compiler_params=_params("arbitrary"),
        name="filters",
    )(*args)


def _hy_proj_kernel(xn_ref, w_ref, cw_ref, u_ref, x0_ref, p_ref, l_ref, s_ref):
    j = pl.program_id(1)
    nj = pl.num_programs(1) - 1
    half = LANES

    @pl.when(j == 0)
    def _():
        p_ref[...] = jnp.zeros(p_ref.shape, F32)
        l_ref[...] = jnp.zeros(l_ref.shape, F32)
        s_ref[...] = jnp.zeros(s_ref.shape, F32)

    p_cur = jnp.dot(w_ref[...], xn_ref[0], preferred_element_type=F32)
    s_cur = pltpu.roll(p_cur[:, :half], half - 1, 1)

    lane = lax.broadcasted_iota(jnp.int32, (1, half), 1)
    first = lane == 0
    last = lane == half - 1
    s_right = jnp.where(j < nj, s_cur, 0.0)

    def conv(rows):
        p_lo = p_ref[rows, :half]
        p_hi = p_ref[rows, half:]
        r_lo = pltpu.roll(p_lo, 1, 1)
        r_hi = pltpu.roll(p_hi, 1, 1)
        s_hi = pltpu.roll(p_hi, half - 1, 1)
        w0, w1, w2, bias = (cw_ref[k, rows, :] for k in range(4))
        lo = w0 * jnp.where(first, l_ref[rows, :], r_lo) + w1 * p_lo + w2 * jnp.where(last, s_hi, s_ref[rows, :]) + bias
        hi = w0 * jnp.where(first, r_lo, r_hi) + w1 * p_hi + w2 * jnp.where(last, s_right[rows], s_hi) + bias
        l_ref[rows, :] = r_hi
        return lo, hi

    for r in range(0, D_HYENA, HY_ROWS):
        x0 = conv(slice(r, r + HY_ROWS))
        x1 = conv(slice(D_HYENA + r, D_HYENA + r + HY_ROWS))
        v = conv(slice(2 * D_HYENA + r, 2 * D_HYENA + r + HY_ROWS))
        for h in range(2):
            cols = slice(h * half, (h + 1) * half)
            x0_ref[0, 0, r:r + HY_ROWS, cols] = x0[h]
            u_ref[0, 0, r:r + HY_ROWS, cols] = v[h] * x1[h]
    p_ref[...] = p_cur
    s_ref[...] = s_cur


def _hy_proj(xn, w_hy_t, cw):
    n_seq, _, seq_len = xn.shape
    tb = TOEPLITZ_BLOCK
    nj = seq_len // tb
    out = jax.ShapeDtypeStruct((nj, n_seq, D_HYENA, tb), F32)
    out_spec = pl.BlockSpec((1, 1, D_HYENA, tb), lambda b, j: (jnp.maximum(j - 1, 0), b, 0, 0))
    return pl.pallas_call(
        _hy_proj_kernel,
        grid=(n_seq, nj + 1),
        in_specs=[
            pl.BlockSpec((1, D_MODEL, tb), lambda b, j: (b, 0, jnp.minimum(j, nj - 1))),
            pl.BlockSpec((3 * D_HYENA, D_MODEL), lambda b, j: (0, 0)),
            pl.BlockSpec((4, 3 * D_HYENA, LANES), lambda b, j: (0, 0, 0)),
        ],
        out_specs=[out_spec, out_spec],
        out_shape=[out, out],
        scratch_shapes=[pltpu.VMEM((3 * D_HYENA, tb), F32), pltpu.VMEM((3 * D_HYENA, LANES), F32),
                        pltpu.VMEM((3 * D_HYENA, LANES), F32)],
        compiler_params=_params("arbitrary", "arbitrary"),
        name="hy_proj",
    )(xn, w_hy_t, cw)


def _longconv_kernel(k_ref, u_ref, x0_ref, d_ref, o_ref, ut_ref, yt_ref, s_ref, ub_ref, y_ref, *, n_seq):
    tb = TOEPLITZ_BLOCK
    half = tb // 2
    rows, n_ch, _ = u_ref.shape
    nj = rows // n_seq
    seq_len = nj * tb
    step = math.gcd(n_seq, ROW_TILE)
    n_phase = ROW_TILE // step

    tri = (lax.broadcasted_iota(jnp.int32, (half, half), 1) >= lax.broadcasted_iota(jnp.int32, (half, half), 0))
    s_ref[rows:, :] = jnp.zeros((2 * ROW_TILE, tb), F32)
    ut_ref[...] = pltpu.einshape("rcl->crl", u_ref[...])

    def channel(i, carry):
        def circ(q):
            seg = jnp.broadcast_to(k_ref[i, :, half * q:half * (q + 1)], (half, half))
            return pltpu.roll(seg, 0, 1, stride=1, stride_axis=0)

        s_ref[0:rows, :] = ut_ref[i]
        for p in range(n_phase):
            ub_ref[p] = s_ref[p * step:p * step + rows + ROW_TILE, :].astype(BF16)
        y_ref[...] = jnp.zeros(y_ref.shape, F32)

        q = (seq_len - tb * (nj - 1)) // half
        c_prev = circ(q - 1)
        g_prev = jnp.where(tri, c_prev, circ(q - 2)).astype(BF16)
        for d in range(-(nj - 1), nj):
            c_a = circ(q)
            c_b = circ(q + 1)
            g_a = jnp.where(tri, c_a, c_prev).astype(BF16)
            g_b = jnp.where(tri, c_b, c_a).astype(BF16)
            tile = jnp.concatenate(
                [jnp.concatenate([g_a, g_b], axis=1), jnp.concatenate([g_prev, g_a], axis=1)], axis=0)
            c_prev, g_prev = c_b, g_b
            q += 2
            n = -(-n_seq * (nj - abs(d)) // ROW_TILE) * ROW_TILE
            src = 0 if d >= 0 else n_seq * -d
            dst = n_seq * d if d >= 0 else 0
            ps, pd = (src % ROW_TILE) // step, (dst % ROW_TILE) // step
            src -= ps * step
            dst += ROW_TILE - pd * step
            y_ref[pd, dst:dst + n, :] += jnp.dot(ub_ref[ps, src:src + n, :], tile, preferred_element_type=F32)

        y = y_ref[0, ROW_TILE:ROW_TILE + rows, :]
        for p in range(1, n_phase):
            y = y + y_ref[p, ROW_TILE - p * step:ROW_TILE - p * step + rows, :]
        yt_ref[i] = y
        return carry

    lax.fori_loop(0, n_ch, channel, 0, unroll=2)
    y = pltpu.einshape("crl->rcl", yt_ref[...])
    o_ref[...] = (y + u_ref[...] * d_ref[...][None]) * x0_ref[...]


def _longconv(kfull, u, x0, d_skip):
    nj, n_seq, n_ch, tb = u.shape
    rows = nj * n_seq
    seq_len = nj * tb
    n_phase = ROW_TILE // math.gcd(n_seq, ROW_TILE)
    blk = pl.BlockSpec((rows, SUBLANES, tb), lambda c: (0, c, 0))
    out = pl.pallas_call(
        functools.partial(_longconv_kernel, n_seq=n_seq),
        grid=(n_ch // SUBLANES,),
        in_specs=[
            pl.BlockSpec((SUBLANES, 1, 2 * seq_len), lambda c: (c, 0, 0)),
            blk, blk,
            pl.BlockSpec((SUBLANES, tb), lambda c: (c, 0)),
        ],
        out_specs=blk,
        out_shape=jax.ShapeDtypeStruct((rows, n_ch, tb), F32),
        scratch_shapes=[pltpu.VMEM((SUBLANES, rows, tb), F32), pltpu.VMEM((SUBLANES, rows, tb), F32),
                        pltpu.VMEM((rows + 2 * ROW_TILE, tb), F32), pltpu.VMEM((n_phase, rows + ROW_TILE, tb), BF16),
                        pltpu.VMEM((n_phase, rows + 2 * ROW_TILE, tb), F32)],
        compiler_params=_params("arbitrary"),
        name="longconv",
    )(kfull.reshape(n_ch, 1, 2 * seq_len), u.reshape(rows, n_ch, tb), x0.reshape(rows, n_ch, tb),
      jnp.broadcast_to(d_skip[:, None], (n_ch, tb)))
    return out.reshape(nj, n_seq, n_ch, tb)


def _proj_conf_kernel(*refs, n_x, n_first, nt):
    x_refs = refs[:n_x]
    (g_ref, w_ref, dw_ref, vec_ref, pw_ref, rg_ref, gate_ref, xn_ref, ycf_ref,
     stage_ref, gp_ref, ext_ref, yb_ref, xb_ref) = refs[n_x:]
    s = pl.program_id(0)
    rows = PROJ_ROWS
    half = (CONF_K - 1) // 2
    n_ext = rows + 2 * CONF_HALO

    @pl.when(s == 0)
    def _():
        stage_ref[...] = jnp.zeros(stage_ref.shape, F32)
        gp_ref[...] = jnp.zeros(gp_ref.shape, F32)
        xb_ref[...] = jnp.zeros(xb_ref.shape, BF16)

    p = jnp.dot(xb_ref[...], w_ref[...], preferred_element_type=F32)
    rg_ref[0] = p[:, 2 * D_CONF:2 * D_CONF + D_RG].astype(BF16)
    gate_ref[0] = p[:, 2 * D_CONF + D_RG:].astype(BF16)
    g_cur = p[:, :D_CONF] * _sigmoid(p[:, D_CONF:2 * D_CONF])

    b = jnp.minimum(s, pl.num_programs(0) - 4) // nt
    xn = _rms(_x_tile(x_refs, b, n_first), g_ref[...])
    xn_ref[0] = xn.T.astype(BF16)

    for r in range(1, SUBLANES):
        ext_ref[r, 0:n_ext - SUBLANES, :] = stage_ref[r:r + n_ext - SUBLANES, :]
    vec = vec_ref[...]
    dw_b, ln_g, ln_b, pw_b = vec[0:1], vec[1:2], vec[2:3], vec[3:4]
    dw = dw_ref[...]
    for c in range(rows // CONF_SUB):
        acc = jnp.broadcast_to(dw_b, (CONF_SUB, D_CONF))
        for k in range(CONF_K):
            off = CONF_HALO - half + k
            start = c * CONF_SUB + off - off % SUBLANES
            if off % SUBLANES == 0:
                tap = stage_ref[start:start + CONF_SUB, :]
            else:
                tap = ext_ref[off % SUBLANES, start:start + CONF_SUB, :]
            acc = acc + dw[k:k + 1] * tap
        mu = jnp.mean(acc, axis=-1, keepdims=True)
        xc = acc - mu
        var = jnp.mean(xc * xc, axis=-1, keepdims=True)
        y = xc * lax.rsqrt(var + EPS) * ln_g + ln_b
        yb_ref[c * CONF_SUB:(c + 1) * CONF_SUB, :] = (y * _sigmoid(y)).astype(BF16)
    ycf_ref[0] = (jnp.dot(yb_ref[...], pw_ref[...], preferred_element_type=F32) + pw_b).astype(ycf_ref.dtype)

    t_next = (s + 2 * nt - 2) % nt
    tail = stage_ref[rows:rows + CONF_HALO, :]
    stage_ref[0:CONF_HALO, :] = jnp.where(t_next > 0, tail, 0.0)
    stage_ref[CONF_HALO:CONF_HALO + rows, :] = gp_ref[...]
    stage_ref[CONF_HALO + rows:, :] = jnp.where(t_next < nt - 1, g_cur[0:CONF_HALO], 0.0)
    gp_ref[...] = g_cur
    xb_ref[...] = xn.astype(BF16)


def _proj_conf(xs, norm_g, w_rest, dw_w, vec, pw_w):
    n_seq = sum(x.shape[0] for x in xs)
    seq_len = xs[0].shape[1]
    n_col = w_rest.shape[1]
    nt = seq_len // PROJ_ROWS
    n_tiles = n_seq * nt
    at = lambda lag: lambda s: divmod(jnp.clip(s - lag, 0, n_tiles - 1), nt)
    cur, prev, old = at(0), at(1), at(3)
    row = lambda w, at: pl.BlockSpec((1, PROJ_ROWS, w), lambda s: (*at(s), 0))
    full = lambda a: pl.BlockSpec(a.shape, lambda s: (0, 0))
    return pl.pallas_call(
        functools.partial(_proj_conf_kernel, n_x=len(xs), n_first=xs[0].shape[0], nt=nt),
        grid=(n_tiles + 3,),
        in_specs=_x_specs(xs, PROJ_ROWS, cur, False) + [full(a) for a in (norm_g, w_rest, dw_w, vec, pw_w)],
        out_specs=[row(D_RG, prev), row(D_MIX, prev),
                   pl.BlockSpec((1, D_MODEL, PROJ_ROWS), lambda s: (cur(s)[0], 0, cur(s)[1])),
                   row(D_CONF, old)],
        out_shape=[jax.ShapeDtypeStruct((n_seq, seq_len, D_RG), BF16),
                   jax.ShapeDtypeStruct((n_seq, seq_len, D_MIX), BF16),
                   jax.ShapeDtypeStruct((n_seq, D_MODEL, seq_len), BF16),
                   jax.ShapeDtypeStruct((n_seq, seq_len, D_CONF), BF16)],
        scratch_shapes=[pltpu.VMEM((PROJ_ROWS + 2 * CONF_HALO, D_CONF), F32),
                        pltpu.VMEM((PROJ_ROWS, D_CONF), F32),
                        pltpu.VMEM((SUBLANES, PROJ_ROWS + 2 * CONF_HALO, D_CONF), F32),
                        pltpu.VMEM((PROJ_ROWS, D_CONF), BF16), pltpu.VMEM((PROJ_ROWS, D_MODEL), BF16)],
        compiler_params=_params("arbitrary"),
        name="proj_conf",
    )(*xs, norm_g, w_rest, dw_w, vec, pw_w)


def _shift_sub(x, s, fill, reverse):
    r = x.shape[1]
    sub = lax.broadcasted_iota(jnp.int32, (1, r, 1), 1)
    if reverse:
        return jnp.where(sub < r - s, pltpu.roll(x, r - s, 1), fill)
    return jnp.where(sub >= s, pltpu.roll(x, s, 1), fill)


def _scan_chunk(a, b, carry, reverse):
    rows, lanes = a.shape
    n = rows // SUBLANES
    a = a.reshape(n, SUBLANES, lanes)
    b = b.reshape(n, SUBLANES, lanes)
    s = 1
    while s < SUBLANES:
        b = b + a * _shift_sub(b, s, 0.0, reverse)
        a = a * _shift_sub(a, s, 1.0, reverse)
        s *= 2
    edge = 0 if reverse else SUBLANES - 1
    at = jnp.broadcast_to(a[:, edge:edge + 1, :], a.shape)
    bt = jnp.broadcast_to(b[:, edge:edge + 1, :], b.shape)
    h_in = [None] * n
    for v in (range(n - 1, -1, -1) if reverse else range(n)):
        h_in[v] = carry
        carry = bt[v] + at[v] * carry
    h = b + a * jnp.stack(h_in)
    return h.reshape(rows, lanes), carry


def _rglru_kernel(p_ref, cw_ref, w_ref, vec_ref, o_ref, xp_ref, xr_ref, hf_ref):
    seq_len = p_ref.shape[1]
    lanes = p_ref.shape[2]
    n_chunk = seq_len // RG_CHUNK
    zeros = jnp.zeros((RG_PAD, lanes), F32)
    xp_ref[0:RG_PAD, :] = zeros
    xp_ref[RG_PAD + seq_len:, :] = zeros
    xp_ref[RG_PAD:RG_PAD + seq_len, :] = p_ref[0].astype(F32)
    cw = cw_ref[...]
    vec = vec_ref[0]
    conv_b = vec[0:1]

    def gates(xr, d):
        g = jnp.dot(xr.astype(BF16), w_ref[0, d], preferred_element_type=F32)
        r = _sigmoid(g[:, :lanes] + vec[1 + 4 * d:2 + 4 * d])
        i = _sigmoid(g[:, lanes:] + vec[2 + 4 * d:3 + 4 * d])
        sp = jax.nn.softplus(-vec[3 + 4 * d:4 + 4 * d])
        a = jnp.exp(-RG_C * r * sp)
        m2 = 1.0 - a * a
        return a, jnp.where(m2 == 0.0, 0.0, m2 * lax.rsqrt(m2)) * (i * xr)

    def fwd(c, carry):
        t0 = pl.multiple_of(c * RG_CHUNK, RG_CHUNK)
        x = xp_ref[pl.ds(t0, RG_CHUNK + 2 * RG_PAD), :]
        xr = conv_b
        for k in range(4):
            xr = xr + cw[k:k + 1] * x[RG_PAD - 2 + k:RG_PAD - 2 + k + RG_CHUNK]
        xr_ref[pl.ds(t0, RG_CHUNK), :] = xr
        a, bt = gates(xr, 0)
        h, carry = _scan_chunk(a, bt, carry, False)
        hf_ref[pl.ds(t0, RG_CHUNK), :] = h
        return carry

    lax.fori_loop(0, n_chunk, fwd, jnp.zeros((SUBLANES, lanes), F32), unroll=2)

    def bwd(i, carry):
        t0 = pl.multiple_of((n_chunk - 1 - i) * RG_CHUNK, RG_CHUNK)
        a, bt = gates(xr_ref[pl.ds(t0, RG_CHUNK), :], 1)
        h, carry = _scan_chunk(a, bt, carry, True)
        o_ref[0, pl.ds(t0, RG_CHUNK), :] = (hf_ref[pl.ds(t0, RG_CHUNK), :] + h).astype(o_ref.dtype)
        return carry

    lax.fori_loop(0, n_chunk, bwd, jnp.zeros((SUBLANES, lanes), F32), unroll=2)


def _rglru(p_rg, conv_w, w_cat, vec):
    n_seq, seq_len, _ = p_rg.shape
    n_cb = D_RG // LANES
    blk = pl.BlockSpec((1, seq_len, LANES), lambda b, c: (b, 0, c))
    return pl.pallas_call(
        _rglru_kernel,
        grid=(n_seq, n_cb),
        in_specs=[
            blk,
            pl.BlockSpec((4, LANES), lambda b, c: (0, c)),
            pl.BlockSpec((1, 2, LANES, 2 * LANES), lambda b, c: (c, 0, 0, 0)),
            pl.BlockSpec((1, 2 * SUBLANES, LANES), lambda b, c: (c, 0, 0)),
        ],
        out_specs=blk,
        out_shape=jax.ShapeDtypeStruct((n_seq, seq_len, D_RG), BF16),
        scratch_shapes=[pltpu.VMEM((seq_len + 2 * RG_PAD, LANES), F32), pltpu.VMEM((seq_len, LANES), F32),
                        pltpu.VMEM((seq_len, LANES), F32)],
        compiler_params=_params("arbitrary", "arbitrary"),
        name="rglru",
    )(p_rg, conv_w, w_cat, vec)


def _out_proj_kernel(yh_ref, ycf_ref, yrg_ref, gate_ref, *refs, n_x, n_first):
    x_refs = refs[:n_x]
    gg_ref, w_ref, fg_ref = refs[n_x:n_x + 3]
    o_refs = refs[n_x + 3:]
    b = pl.program_id(1)
    gg = gg_ref[...]
    yh = jnp.concatenate([yh_ref[k, 0].T for k in range(yh_ref.shape[0])], axis=0)
    y = jnp.concatenate([
        _rms(yh, gg[:, :D_HYENA]),
        _rms(ycf_ref[0].astype(F32), gg[:, D_HYENA:D_HYENA + D_CONF]),
        _rms(yrg_ref[0].astype(F32), gg[:, D_HYENA + D_CONF:]),
    ], axis=-1)
    gate = gate_ref[0].astype(F32)
    y = y * (gate * _sigmoid(gate))
    out = _x_tile(x_refs, b, n_first) + jnp.dot(y.astype(BF16), w_ref[...], preferred_element_type=F32)
    if len(o_refs) == 1:
        o_refs[0][0] = out
        return
    out = _rms(out, fg_ref[...])

    @pl.when(b < n_first)
    def _():
        o_refs[0][0] = out

    @pl.when(b >= n_first)
    def _():
        o_refs[1][0] = out


def _out_proj(y_hy, y_cf, y_rg, gate, xs, grp_g, w_out, final_g, n_first, split):
    nj, n_seq, n_ch, tb = y_hy.shape
    seq_len = nj * tb
    per = OUT_ROWS // tb
    tok = lambda w: pl.BlockSpec((1, OUT_ROWS, w), lambda j, b: (b, j, 0))
    vec = lambda w: pl.BlockSpec((1, w), lambda j, b: (0, 0))
    if not split:
        out_specs = [tok(D_MODEL)]
        out_shape = [jax.ShapeDtypeStruct((n_seq, seq_len, D_MODEL), F32)]
    else:
        out_specs = [
            pl.BlockSpec((1, OUT_ROWS, D_MODEL), lambda j, b: (jnp.minimum(b, n_first - 1), j, 0)),
            pl.BlockSpec((1, OUT_ROWS, D_MODEL), lambda j, b: (jnp.maximum(b - n_first, 0), j, 0)),
        ]
        out_shape = [jax.ShapeDtypeStruct((n, seq_len, D_MODEL), F32) for n in (n_first, n_seq - n_first)]
    return pl.pallas_call(
        functools.partial(_out_proj_kernel, n_x=len(xs), n_first=n_first),
        grid=(nj // per, n_seq),
        in_specs=[
            pl.BlockSpec((per, 1, n_ch, tb), lambda j, b: (j, b, 0, 0)),
            tok(D_CONF), tok(D_RG), tok(D_MIX),
        ] + _x_specs(xs, OUT_ROWS, lambda j, b: (b, j), True) + [
            vec(D_MIX), pl.BlockSpec((D_MIX, D_MODEL), lambda j, b: (0, 0)), vec(D_MODEL),
        ],
        out_specs=out_specs,
        out_shape=out_shape,
        compiler_params=_params("arbitrary", "arbitrary"),
        name="out_proj",
    )(y_hy, y_cf, y_rg, gate, *xs, grp_g, w_out, final_g)


def _rg_gate_weights(wa, wx):
    per = LANES // GROUP_WIDTH
    n_cb = D_RG // LANES

    def diag(w):
        w = w.reshape(n_cb, per, GROUP_WIDTH, GROUP_WIDTH)
        eye = jnp.eye(per, dtype=w.dtype)
        return jnp.einsum("cpij,pq->cpiqj", w, eye).reshape(n_cb, LANES, LANES)

    dirs = [jnp.concatenate([diag(wa[d]), diag(wx[d])], axis=-1) for d in range(2)]
    return jnp.stack(dirs, axis=1).astype(BF16)


def _layer(xs, n_first, split, final_g, norm_g, w_in, hy_conv_w, hy_conv_b, hy_w1, hy_b1, hy_w2, hy_b2, hy_w3,
           hy_b3, hy_w4, hy_freq, hy_d, cf_dw_w, cf_dw_b, cf_ln_g, cf_ln_b, cf_pw_w, cf_pw_b, rg_conv_w,
           rg_conv_b, rg_wa, rg_ba, rg_wx, rg_bx, rg_lam, grp_g, w_out):
    seq_len = xs[0].shape[1]
    s1 = 3 * D_HYENA
    g_row = norm_g.reshape(1, D_MODEL)

    kfull = _filters(hy_w1, hy_b1, hy_w2, hy_b2, hy_w3, hy_b3, hy_w4, hy_freq, seq_len)
    cw = jnp.concatenate([hy_conv_w, hy_conv_b[None]], axis=0)
    cw = jnp.broadcast_to(cw[:, :, None], (4, s1, LANES))
    zrow = jnp.zeros((D_CONF,), F32)
    cf_vec = jnp.stack([cf_dw_b, cf_ln_g, cf_ln_b, cf_pw_b, zrow, zrow, zrow, zrow])
    p_rg, gate, xn, y_cf = _proj_conf(xs, g_row, w_in[:, s1:].astype(BF16), cf_dw_w, cf_vec,
                                      cf_pw_w.astype(BF16))
    u, x0 = _hy_proj(xn, w_in[:, :s1].T.astype(BF16), cw)
    y_hy = _longconv(kfull, u, x0, hy_d)

    n_cb = D_RG // LANES
    zr = jnp.zeros((D_RG,), F32)
    rg_rows = [rg_conv_b, rg_ba[0], rg_bx[0], rg_lam[0], zr, rg_ba[1], rg_bx[1], rg_lam[1]] + [zr] * 8
    rg_vec = jnp.stack(rg_rows).reshape(2 * SUBLANES, n_cb, LANES).transpose(1, 0, 2)
    y_rg = _rglru(p_rg, rg_conv_w, _rg_gate_weights(rg_wa, rg_wx), rg_vec)

    return _out_proj(y_hy, y_cf, y_rg, gate, xs, grp_g.reshape(1, D_MIX), w_out.astype(BF16),
                     final_g.reshape(1, D_MODEL), n_first, split)


def kernel(x_prompt, x_sample, norm_g, w_in, hy_conv_w, hy_conv_b, hy_w1, hy_b1, hy_w2, hy_b2, hy_w3, hy_b3, hy_w4, hy_freq, hy_d, cf_dw_w, cf_dw_b, cf_ln_g, cf_ln_b, cf_pw_w, cf_pw_b, rg_conv_w, rg_conv_b, rg_wa, rg_ba, rg_wx, rg_bx, rg_lam, grp_g, w_out, final_g):
    layer_params = (norm_g, w_in, hy_conv_w, hy_conv_b, hy_w1, hy_b1, hy_w2, hy_b2, hy_w3, hy_b3, hy_w4, hy_freq,
                    hy_d, cf_dw_w, cf_dw_b, cf_ln_g, cf_ln_b, cf_pw_w, cf_pw_b, rg_conv_w, rg_conv_b, rg_wa,
                    rg_ba, rg_wx, rg_bx, rg_lam, grp_g, w_out)
    assert x_prompt.shape[1:] == x_sample.shape[1:]
    n_prompt = x_prompt.shape[0]
    n_rows = (n_prompt + x_sample.shape[0]) * (x_prompt.shape[1] // TOEPLITZ_BLOCK)
    assert n_rows % ROW_TILE == 0, "longconv needs (sequences x time blocks) to fill whole bf16 row tiles"
    depth = norm_g.shape[0]
    xs = (x_prompt, x_sample)
    for l in range(depth):
        xs = tuple(_layer(xs, n_prompt, l == depth - 1, final_g, *[w[l] for w in layer_params]))
    return xs
```

```python
import functools
import math

import jax
import jax.numpy as jnp
from jax import lax
from jax.experimental import pallas as pl
from jax.experimental.pallas import tpu as pltpu

F32 = jnp.float32
BF16 = jnp.bfloat16

D_MODEL = 1024
D_HYENA = 384
D_CONF = 256
D_RG = 384
D_MIX = D_HYENA + D_CONF + D_RG
GROUP_WIDTH = 64
HYENA_BANDS = 16
HYENA_ORDER = 64
HYENA_TARGET = 1e-2
HYENA_FAST_DECAY_PCT = 0.3
HYENA_SLOW_DECAY_PCT = 1.5
CONF_K = 31
RG_C = 8.0
EPS = 1e-6

LANES = 128
SUBLANES = 8
MXU_DIM = 256
VMEM_LIMIT_BYTES = 56 * 1024 * 1024

TOEPLITZ_BLOCK = MXU_DIM
ROW_TILE = 16
FILTER_TILE = 2048
PROJ_ROWS = 512
CONF_HALO = 16
CONF_SUB = 64
OUT_ROWS = 512
HY_ROWS = 128
RG_CHUNK = 256
RG_PAD = 8


def _params(*sem):
    return pltpu.CompilerParams(dimension_semantics=sem, vmem_limit_bytes=VMEM_LIMIT_BYTES)


def _rms(x, g):
    return x * lax.rsqrt(jnp.mean(x * x, axis=-1, keepdims=True) + EPS) * g


def _sigmoid(x):
    return 0.5 * jnp.tanh(0.5 * x) + 0.5


def _x_tile(x_refs, b, n_first):
    if len(x_refs) == 1:
        return x_refs[0][0]
    return jnp.where(b < n_first, x_refs[0][0], x_refs[1][0])


def _x_specs(xs, rows, seq_tile, hold_tile):
    if len(xs) == 1:
        return [pl.BlockSpec((1, rows, D_MODEL), lambda *g: (*seq_tile(*g), 0))]
    n1 = xs[0].shape[0]
    last = xs[0].shape[1] // rows - 1

    def first(*g):
        b, t = seq_tile(*g)
        return (jnp.minimum(b, n1 - 1), t if hold_tile else jnp.where(b < n1, t, last), 0)

    def second(*g):
        b, t = seq_tile(*g)
        return (jnp.maximum(b - n1, 0), t if hold_tile else jnp.where(b < n1, 0, t), 0)

    return [pl.BlockSpec((1, rows, D_MODEL), first), pl.BlockSpec((1, rows, D_MODEL), second)]


def _filters_kernel(w1t_ref, w1c_ref, w1s_ref, b1_ref, w2_ref, b2_ref, w3_ref, b3_ref, w4_ref,
                    fr_ref, o_ref, *, seq_len):
    i = pl.program_id(0)
    tile = o_ref.shape[1]
    m = i * tile + lax.broadcasted_iota(jnp.int32, (1, tile), 1)
    pos = jnp.abs(m - seq_len).astype(F32)
    t = pos / (seq_len - 1.0)
    w = (2.0 * math.pi / seq_len) * pos
    band = lax.broadcasted_iota(jnp.int32, (HYENA_BANDS, 1), 0).astype(F32)
    f = 1e-4 + band * ((HYENA_BANDS - 1 - 1e-4) / (HYENA_BANDS - 1))
    fw = f * w
    hi = lax.Precision.HIGHEST
    fr = fr_ref[...]
    pre = (w1t_ref[...] * t + jnp.dot(w1c_ref[...], jnp.cos(fw), precision=hi)
           - jnp.dot(w1s_ref[...], jnp.sin(fw), precision=hi))
    h = jnp.sin(fr * (pre + b1_ref[...]))
    h = jnp.sin(fr * (jnp.dot(w2_ref[...], h, precision=hi) + b2_ref[...]))
    h = jnp.sin(fr * (jnp.dot(w3_ref[...], h, precision=hi) + b3_ref[...]))
    k = jnp.dot(w4_ref[...], h, precision=hi)
    max_decay = math.log(HYENA_TARGET) / HYENA_FAST_DECAY_PCT
    min_decay = math.log(HYENA_TARGET) / HYENA_SLOW_DECAY_PCT
    ch = lax.broadcasted_iota(jnp.int32, (D_HYENA, 1), 0).astype(F32)
    delta = jnp.abs(min_decay + ch * ((max_decay - min_decay) / (D_HYENA - 1)))
    decay = jnp.exp(-t * delta)
    k = jnp.where(m >= seq_len, k[:D_HYENA], k[D_HYENA:]) * decay
    o_ref[...] = jnp.where(m == 0, 0.0, k)


def _filters(w1, b1, w2, b2, w3, b3, w4, freq, seq_len):
    w1t = w1.T
    col = lambda v: v.reshape(-1, 1)
    args = (w1t[:, 0:1], w1t[:, 1:1 + HYENA_BANDS], w1t[:, 1 + HYENA_BANDS:], col(b1), w2.T, col(b2),
            w3.T, col(b3), w4.T, col(freq))
    full = lambda a: pl.BlockSpec(a.shape, lambda i: (0, 0))
    return pl.pallas_call(
        functools.partial(_filters_kernel, seq_len=seq_len),
        grid=(2 * seq_len // FILTER_TILE,),
        in_specs=[full(a) for a in args],
        out_specs=pl.BlockSpec((D_HYENA, FILTER_TILE), lambda i: (0, i)),
        out_shape=jax.ShapeDtypeStruct((D_HYENA, 2 * seq_len), F32),
        compiler_params=_params("arbitrary"),
        name="filters",
    )(*args)


def _hy_proj_kernel(xn_ref, w_ref, cw_ref, u_ref, x0_ref, p_ref, l_ref, s_ref):
    j = pl.program_id(1)
    nj = pl.num_programs(1) - 1
    half = LANES

    @pl.when(j == 0)
    def _():
        p_ref[...] = jnp.zeros(p_ref.shape, F32)
        l_ref[...] = jnp.zeros(l_ref.shape, F32)
        s_ref[...] = jnp.zeros(s_ref.shape, F32)

    p_cur = jnp.dot(w_ref[...], xn_ref[0], preferred_element_type=F32)
    s_cur = pltpu.roll(p_cur[:, :half], half - 1, 1)

    lane = lax.broadcasted_iota(jnp.int32, (1, half), 1)
    first = lane == 0
    last = lane == half - 1
    s_right = jnp.where(j < nj, s_cur, 0.0)

    def conv(rows):
        p_lo = p_ref[rows, :half]
        p_hi = p_ref[rows, half:]
        r_lo = pltpu.roll(p_lo, 1, 1)
        r_hi = pltpu.roll(p_hi, 1, 1)
        s_hi = pltpu.roll(p_hi, half - 1, 1)
        w0, w1, w2, bias = (cw_ref[k, rows, :] for k in range(4))
        lo = w0 * jnp.where(first, l_ref[rows, :], r_lo) + w1 * p_lo + w2 * jnp.where(last, s_hi, s_ref[rows, :]) + bias
        hi = w0 * jnp.where(first, r_lo, r_hi) + w1 * p_hi + w2 * jnp.where(last, s_right[rows], s_hi) + bias
        l_ref[rows, :] = r_hi
        return lo, hi

    for r in range(0, D_HYENA, HY_ROWS):
        x0 = conv(slice(r, r + HY_ROWS))
        x1 = conv(slice(D_HYENA + r, D_HYENA + r + HY_ROWS))
        v = conv(slice(2 * D_HYENA + r, 2 * D_HYENA + r + HY_ROWS))
        for h in range(2):
            cols = slice(h * half, (h + 1) * half)
            x0_ref[0, 0, r:r + HY_ROWS, cols] = x0[h]
            u_ref[0, 0, r:r + HY_ROWS, cols] = v[h] * x1[h]
    p_ref[...] = p_cur
    s_ref[...] = s_cur


def _hy_proj(xn, w_hy_t, cw):
    n_seq, _, seq_len = xn.shape
    tb = TOEPLITZ_BLOCK
    nj = seq_len // tb
    out = jax.ShapeDtypeStruct((nj, n_seq, D_HYENA, tb), F32)
    out_spec = pl.BlockSpec((1, 1, D_HYENA, tb), lambda b, j: (jnp.maximum(j - 1, 0), b, 0, 0))
    return pl.pallas_call(
        _hy_proj_kernel,
        grid=(n_seq, nj + 1),
        in_specs=[
            pl.BlockSpec((1, D_MODEL, tb), lambda b, j: (b, 0, jnp.minimum(j, nj - 1))),
            pl.BlockSpec((3 * D_HYENA, D_MODEL), lambda b, j: (0, 0)),
            pl.BlockSpec((4, 3 * D_HYENA, LANES), lambda b, j: (0, 0, 0)),
        ],
        out_specs=[out_spec, out_spec],
        out_shape=[out, out],
        scratch_shapes=[pltpu.VMEM((3 * D_HYENA, tb), F32), pltpu.VMEM((3 * D_HYENA, LANES), F32),
                        pltpu.VMEM((3 * D_HYENA, LANES), F32)],
        compiler_params=_params("arbitrary", "arbitrary"),
        name="hy_proj",
    )(xn, w_hy_t, cw)


def _longconv_kernel(k_ref, u_ref, x0_ref, d_ref, o_ref, ut_ref, yt_ref, s_ref, ub_ref, y_ref, *, n_seq):
    tb = TOEPLITZ_BLOCK
    half = tb // 2
    rows, n_ch, _ = u_ref.shape
    nj = rows // n_seq
    seq_len = nj * tb
    step = math.gcd(n_seq, ROW_TILE)
    n_phase = ROW_TILE // step

    tri = (lax.broadcasted_iota(jnp.int32, (half, half), 1) >= lax.broadcasted_iota(jnp.int32, (half, half), 0))
    s_ref[rows:, :] = jnp.zeros((2 * ROW_TILE, tb), F32)
    ut_ref[...] = pltpu.einshape("rcl->crl", u_ref[...])

    def channel(i, carry):
        def circ(q):
            seg = jnp.broadcast_to(k_ref[i, :, half * q:half * (q + 1)], (half, half))
            return pltpu.roll(seg, 0, 1, stride=1, stride_axis=0)

        s_ref[0:rows, :] = ut_ref[i]
        for p in range(n_phase):
            ub_ref[p] = s_ref[p * step:p * step + rows + ROW_TILE, :].astype(BF16)
        y_ref[...] = jnp.zeros(y_ref.shape, F32)

        q = (seq_len - tb * (nj - 1)) // half
        c_prev = circ(q - 1)
        g_prev = jnp.where(tri, c_prev, circ(q - 2)).astype(BF16)
        for d in range(-(nj - 1), nj):
            c_a = circ(q)
            c_b = circ(q + 1)
            g_a = jnp.where(tri, c_a, c_prev).astype(BF16)
            g_b = jnp.where(tri, c_b, c_a).astype(BF16)
            tile = jnp.concatenate(
                [jnp.concatenate([g_a, g_b], axis=1), jnp.concatenate([g_prev, g_a], axis=1)], axis=0)
            c_prev, g_prev = c_b, g_b
            q += 2
            n = -(-n_seq * (nj - abs(d)) // ROW_TILE) * ROW_TILE
            src = 0 if d >= 0 else n_seq * -d
            dst = n_seq * d if d >= 0 else 0
            ps, pd = (src % ROW_TILE) // step, (dst % ROW_TILE) // step
            src -= ps * step
            dst += ROW_TILE - pd * step
            y_ref[pd, dst:dst + n, :] += jnp.dot(ub_ref[ps, src:src + n, :], tile, preferred_element_type=F32)

        y = y_ref[0, ROW_TILE:ROW_TILE + rows, :]
        for p in range(1, n_phase):
            y = y + y_ref[p, ROW_TILE - p * step:ROW_TILE - p * step + rows, :]
        yt_ref[i] = y
        return carry

    lax.fori_loop(0, n_ch, channel, 0, unroll=2)
    y = pltpu.einshape("crl->rcl", yt_ref[...])
    o_ref[...] = (y + u_ref[...] * d_ref[...][None]) * x0_ref[...]


def _longconv(kfull, u, x0, d_skip):
    nj, n_seq, n_ch, tb = u.shape
    rows = nj * n_seq
    seq_len = nj * tb
    n_phase = ROW_TILE // math.gcd(n_seq, ROW_TILE)
    blk = pl.BlockSpec((rows, SUBLANES, tb), lambda c: (0, c, 0))
    out = pl.pallas_call(
        functools.partial(_longconv_kernel, n_seq=n_seq),
        grid=(n_ch // SUBLANES,),
        in_specs=[
            pl.BlockSpec((SUBLANES, 1, 2 * seq_len), lambda c: (c, 0, 0)),
            blk, blk,
            pl.BlockSpec((SUBLANES, tb), lambda c: (c, 0)),
        ],
        out_specs=blk,
        out_shape=jax.ShapeDtypeStruct((rows, n_ch, tb), F32),
        scratch_shapes=[pltpu.VMEM((SUBLANES, rows, tb), F32), pltpu.VMEM((SUBLANES, rows, tb), F32),
                        pltpu.VMEM((rows + 2 * ROW_TILE, tb), F32), pltpu.VMEM((n_phase, rows + ROW_TILE, tb), BF16),
                        pltpu.VMEM((n_phase, rows + 2 * ROW_TILE, tb), F32)],
        compiler_params=_params("arbitrary"),
        name="longconv",
    )(kfull.reshape(n_ch, 1, 2 * seq_len), u.reshape(rows, n_ch, tb), x0.reshape(rows, n_ch, tb),
      jnp.broadcast_to(d_skip[:, None], (n_ch, tb)))
    return out.reshape(nj, n_seq, n_ch, tb)


def _proj_conf_kernel(*refs, n_x, n_first, nt):
    x_refs = refs[:n_x]
    (g_ref, w_ref, dw_ref, vec_ref, pw_ref, rg_ref, gate_ref, xn_ref, ycf_ref,
     stage_ref, gp_ref, ext_ref, yb_ref, xb_ref) = refs[n_x:]
    s = pl.program_id(0)
    rows = PROJ_ROWS
    half = (CONF_K - 1) // 2
    n_ext = rows + 2 * CONF_HALO

    @pl.when(s == 0)
    def _():
        stage_ref[...] = jnp.zeros(stage_ref.shape, F32)
        gp_ref[...] = jnp.zeros(gp_ref.shape, F32)
        xb_ref[...] = jnp.zeros(xb_ref.shape, BF16)

    p = jnp.dot(xb_ref[...], w_ref[...], preferred_element_type=F32)
    rg_ref[0] = p[:, 2 * D_CONF:2 * D_CONF + D_RG].astype(BF16)
    gate_ref[0] = p[:, 2 * D_CONF + D_RG:].astype(BF16)
    g_cur = p[:, :D_CONF] * _sigmoid(p[:, D_CONF:2 * D_CONF])

    b = jnp.minimum(s, pl.num_programs(0) - 4) // nt
    xn = _rms(_x_tile(x_refs, b, n_first), g_ref[...])
    xn_ref[0] = xn.T.astype(BF16)

    for r in range(1, SUBLANES):
        ext_ref[r, 0:n_ext - SUBLANES, :] = stage_ref[r:r + n_ext - SUBLANES, :]
    vec = vec_ref[...]
    dw_b, ln_g, ln_b, pw_b = vec[0:1], vec[1:2], vec[2:3], vec[3:4]
    dw = dw_ref[...]
    for c in range(rows // CONF_SUB):
        acc = jnp.broadcast_to(dw_b, (CONF_SUB, D_CONF))
        for k in range(CONF_K):
            off = CONF_HALO - half + k
            start = c * CONF_SUB + off - off % SUBLANES
            if off % SUBLANES == 0:
                tap = stage_ref[start:start + CONF_SUB, :]
            else:
                tap = ext_ref[off % SUBLANES, start:start + CONF_SUB, :]
            acc = acc + dw[k:k + 1] * tap
        mu = jnp.mean(acc, axis=-1, keepdims=True)
        xc = acc - mu
        var = jnp.mean(xc * xc, axis=-1, keepdims=True)
        y = xc * lax.rsqrt(var + EPS) * ln_g + ln_b
        yb_ref[c * CONF_SUB:(c + 1) * CONF_SUB, :] = (y * _sigmoid(y)).astype(BF16)
    ycf_ref[0] = (jnp.dot(yb_ref[...], pw_ref[...], preferred_element_type=F32) + pw_b).astype(ycf_ref.dtype)

    t_next = (s + 2 * nt - 2) % nt
    tail = stage_ref[rows:rows + CONF_HALO, :]
    stage_ref[0:CONF_HALO, :] = jnp.where(t_next > 0, tail, 0.0)
    stage_ref[CONF_HALO:CONF_HALO + rows, :] = gp_ref[...]
    stage_ref[CONF_HALO + rows:, :] = jnp.where(t_next < nt - 1, g_cur[0:CONF_HALO], 0.0)
    gp_ref[...] = g_cur
    xb_ref[...] = xn.astype(BF16)


def _proj_conf(xs, norm_g, w_rest, dw_w, vec, pw_w):
    n_seq = sum(x.shape[0] for x in xs)
    seq_len = xs[0].shape[1]
    n_col = w_rest.shape[1]
    nt = seq_len // PROJ_ROWS
    n_tiles = n_seq * nt
    at = lambda lag: lambda s: divmod(jnp.clip(s - lag, 0, n_tiles - 1), nt)
    cur, prev, old = at(0), at(1), at(3)
    row = lambda w, at: pl.BlockSpec((1, PROJ_ROWS, w), lambda s: (*at(s), 0))
    full = lambda a: pl.BlockSpec(a.shape, lambda s: (0, 0))
    return pl.pallas_call(
        functools.partial(_proj_conf_kernel, n_x=len(xs), n_first=xs[0].shape[0], nt=nt),
        grid=(n_tiles + 3,),
        in_specs=_x_specs(xs, PROJ_ROWS, cur, False) + [full(a) for a in (norm_g, w_rest, dw_w, vec, pw_w)],
        out_specs=[row(D_RG, prev), row(D_MIX, prev),
                   pl.BlockSpec((1, D_MODEL, PROJ_ROWS), lambda s: (cur(s)[0], 0, cur(s)[1])),
                   row(D_CONF, old)],
        out_shape=[jax.ShapeDtypeStruct((n_seq, seq_len, D_RG), BF16),
                   jax.ShapeDtypeStruct((n_seq, seq_len, D_MIX), BF16),
                   jax.ShapeDtypeStruct((n_seq, D_MODEL, seq_len), BF16),
                   jax.ShapeDtypeStruct((n_seq, seq_len, D_CONF), BF16)],
        scratch_shapes=[pltpu.VMEM((PROJ_ROWS + 2 * CONF_HALO, D_CONF), F32),
                        pltpu.VMEM((PROJ_ROWS, D_CONF), F32),
                        pltpu.VMEM((SUBLANES, PROJ_ROWS + 2 * CONF_HALO, D_CONF), F32),
                        pltpu.VMEM((PROJ_ROWS, D_CONF), BF16), pltpu.VMEM((PROJ_ROWS, D_MODEL), BF16)],
        compiler_params=_params("arbitrary"),
        name="proj_conf",
    )(*xs, norm_g, w_rest, dw_w, vec, pw_w)


def _shift_sub(x, s, fill, reverse):
    r = x.shape[1]
    sub = lax.broadcasted_iota(jnp.int32, (1, r, 1), 1)
    if reverse:
        return jnp.where(sub < r - s, pltpu.roll(x, r - s, 1), fill)
    return jnp.where(sub >= s, pltpu.roll(x, s, 1), fill)


def _scan_chunk(a, b, carry, reverse):
    rows, lanes = a.shape
    n = rows // SUBLANES
    a = a.reshape(n, SUBLANES, lanes)
    b = b.reshape(n, SUBLANES, lanes)
    s = 1
    while s < SUBLANES:
        b = b + a * _shift_sub(b, s, 0.0, reverse)
        a = a * _shift_sub(a, s, 1.0, reverse)
        s *= 2
    edge = 0 if reverse else SUBLANES - 1
    at = jnp.broadcast_to(a[:, edge:edge + 1, :], a.shape)
    bt = jnp.broadcast_to(b[:, edge:edge + 1, :], b.shape)
    h_in = [None] * n
    for v in (range(n - 1, -1, -1) if reverse else range(n)):
        h_in[v] = carry
        carry = bt[v] + at[v] * carry
    h = b + a * jnp.stack(h_in)
    return h.reshape(rows, lanes), carry


def _rglru_kernel(p_ref, cw_ref, w_ref, vec_ref, o_ref, xp_ref, xr_ref, hf_ref):
    seq_len = p_ref.shape[1]
    lanes = p_ref.shape[2]
    n_chunk = seq_len // RG_CHUNK
    zeros = jnp.zeros((RG_PAD, lanes), F32)
    xp_ref[0:RG_PAD, :] = zeros
    xp_ref[RG_PAD + seq_len:, :] = zeros
    xp_ref[RG_PAD:RG_PAD + seq_len, :] = p_ref[0].astype(F32)
    cw = cw_ref[...]
    vec = vec_ref[0]
    conv_b = vec[0:1]

    def gates(xr, d):
        g = jnp.dot(xr.astype(BF16), w_ref[0, d], preferred_element_type=F32)
        r = _sigmoid(g[:, :lanes] + vec[1 + 4 * d:2 + 4 * d])
        i = _sigmoid(g[:, lanes:] + vec[2 + 4 * d:3 + 4 * d])
        sp = jax.nn.softplus(-vec[3 + 4 * d:4 + 4 * d])
        a = jnp.exp(-RG_C * r * sp)
        m2 = 1.0 - a * a
        return a, jnp.where(m2 == 0.0, 0.0, m2 * lax.rsqrt(m2)) * (i * xr)

    def fwd(c, carry):
        t0 = pl.multiple_of(c * RG_CHUNK, RG_CHUNK)
        n = RG_CHUNK // SUBLANES
        x = xp_ref[pl.ds(t0, RG_CHUNK + 2 * RG_PAD), :].reshape(n + 2, SUBLANES, lanes)
        sub = lax.broadcasted_iota(jnp.int32, (1, SUBLANES, 1), 1)
        xr = conv_b + cw[2:3] * x[1:n + 1].reshape(RG_CHUNK, lanes)
        for k, d in ((0, 2), (1, 1)):
            rolled = pltpu.roll(x[0:n + 1], d, 1)
            tap = jnp.where(sub >= d, rolled[1:], rolled[:n])
            xr = xr + cw[k:k + 1] * tap.reshape(RG_CHUNK, lanes)
        rolled = pltpu.roll(x[1:], SUBLANES - 1, 1)
        tap = jnp.where(sub < SUBLANES - 1, rolled[:n], rolled[1:])
        xr = xr + cw[3:4] * tap.reshape(RG_CHUNK, lanes)
        xr_ref[pl.ds(t0, RG_CHUNK), :] = xr
        a, bt = gates(xr, 0)
        h, carry = _scan_chunk(a, bt, carry, False)
        hf_ref[pl.ds(t0, RG_CHUNK), :] = h
        return carry

    lax.fori_loop(0, n_chunk, fwd, jnp.zeros((SUBLANES, lanes), F32), unroll=2)

    def bwd(i, carry):
        t0 = pl.multiple_of((n_chunk - 1 - i) * RG_CHUNK, RG_CHUNK)
        a, bt = gates(xr_ref[pl.ds(t0, RG_CHUNK), :], 1)
        h, carry = _scan_chunk(a, bt, carry, True)
        o_ref[0, pl.ds(t0, RG_CHUNK), :] = (hf_ref[pl.ds(t0, RG_CHUNK), :] + h).astype(o_ref.dtype)
        return carry

    lax.fori_loop(0, n_chunk, bwd, jnp.zeros((SUBLANES, lanes), F32), unroll=4)


def _rglru(p_rg, conv_w, w_cat, vec):
    n_seq, seq_len, _ = p_rg.shape
    n_cb = D_RG // LANES
    blk = pl.BlockSpec((1, seq_len, LANES), lambda b, c: (b, 0, c))
    return pl.pallas_call(
        _rglru_kernel,
        grid=(n_seq, n_cb),
        in_specs=[
            blk,
            pl.BlockSpec((4, LANES), lambda b, c: (0, c)),
            pl.BlockSpec((1, 2, LANES, 2 * LANES), lambda b, c: (c, 0, 0, 0)),
            pl.BlockSpec((1, 2 * SUBLANES, LANES), lambda b, c: (c, 0, 0)),
        ],
        out_specs=blk,
        out_shape=jax.ShapeDtypeStruct((n_seq, seq_len, D_RG), BF16),
        scratch_shapes=[pltpu.VMEM((seq_len + 2 * RG_PAD, LANES), F32), pltpu.VMEM((seq_len, LANES), F32),
                        pltpu.VMEM((seq_len, LANES), F32)],
        compiler_params=_params("arbitrary", "arbitrary"),
        name="rglru",
    )(p_rg, conv_w, w_cat, vec)


def _out_proj_kernel(yh_ref, ycf_ref, yrg_ref, gate_ref, *refs, n_x, n_first):
    x_refs = refs[:n_x]
    gg_ref, w_ref, fg_ref = refs[n_x:n_x + 3]
    o_refs = refs[n_x + 3:]
    b = pl.program_id(1)
    gg = gg_ref[...]
    yh = jnp.concatenate([yh_ref[k, 0].T for k in range(yh_ref.shape[0])], axis=0)
    y = jnp.concatenate([
        _rms(yh, gg[:, :D_HYENA]),
        _rms(ycf_ref[0].astype(F32), gg[:, D_HYENA:D_HYENA + D_CONF]),
        _rms(yrg_ref[0].astype(F32), gg[:, D_HYENA + D_CONF:]),
    ], axis=-1)
    gate = gate_ref[0].astype(F32)
    y = y * (gate * _sigmoid(gate))
    out = _x_tile(x_refs, b, n_first) + jnp.dot(y.astype(BF16), w_ref[...], preferred_element_type=F32)
    if len(o_refs) == 1:
        o_refs[0][0] = out
        return
    out = _rms(out, fg_ref[...])

    @pl.when(b < n_first)
    def _():
        o_refs[0][0] = out

    @pl.when(b >= n_first)
    def _():
        o_refs[1][0] = out


def _out_proj(y_hy, y_cf, y_rg, gate, xs, grp_g, w_out, final_g, n_first, split):
    nj, n_seq, n_ch, tb = y_hy.shape
    seq_len = nj * tb
    per = OUT_ROWS // tb
    tok = lambda w: pl.BlockSpec((1, OUT_ROWS, w), lambda j, b: (b, j, 0))
    vec = lambda w: pl.BlockSpec((1, w), lambda j, b: (0, 0))
    if not split:
        out_specs = [tok(D_MODEL)]
        out_shape = [jax.ShapeDtypeStruct((n_seq, seq_len, D_MODEL), F32)]
    else:
        out_specs = [
            pl.BlockSpec((1, OUT_ROWS, D_MODEL), lambda j, b: (jnp.minimum(b, n_first - 1), j, 0)),
            pl.BlockSpec((1, OUT_ROWS, D_MODEL), lambda j, b: (jnp.maximum(b - n_first, 0), j, 0)),
        ]
        out_shape = [jax.ShapeDtypeStruct((n, seq_len, D_MODEL), F32) for n in (n_first, n_seq - n_first)]
    return pl.pallas_call(
        functools.partial(_out_proj_kernel, n_x=len(xs), n_first=n_first),
        grid=(nj // per, n_seq),
        in_specs=[
            pl.BlockSpec((per, 1, n_ch, tb), lambda j, b: (j, b, 0, 0)),
            tok(D_CONF), tok(D_RG), tok(D_MIX),
        ] + _x_specs(xs, OUT_ROWS, lambda j, b: (b, j), True) + [
            vec(D_MIX), pl.BlockSpec((D_MIX, D_MODEL), lambda j, b: (0, 0)), vec(D_MODEL),
        ],
        out_specs=out_specs,
        out_shape=out_shape,
        compiler_params=_params("arbitrary", "arbitrary"),
        name="out_proj",
    )(y_hy, y_cf, y_rg, gate, *xs, grp_g, w_out, final_g)


def _rg_gate_weights(wa, wx):
    per = LANES // GROUP_WIDTH
    n_cb = D_RG // LANES

    def diag(w):
        w = w.reshape(n_cb, per, GROUP_WIDTH, GROUP_WIDTH)
        eye = jnp.eye(per, dtype=w.dtype)
        return jnp.einsum("cpij,pq->cpiqj", w, eye).reshape(n_cb, LANES, LANES)

    dirs = [jnp.concatenate([diag(wa[d]), diag(wx[d])], axis=-1) for d in range(2)]
    return jnp.stack(dirs, axis=1).astype(BF16)


def _layer(xs, n_first, split, final_g, norm_g, w_in, hy_conv_w, hy_conv_b, hy_w1, hy_b1, hy_w2, hy_b2, hy_w3,
           hy_b3, hy_w4, hy_freq, hy_d, cf_dw_w, cf_dw_b, cf_ln_g, cf_ln_b, cf_pw_w, cf_pw_b, rg_conv_w,
           rg_conv_b, rg_wa, rg_ba, rg_wx, rg_bx, rg_lam, grp_g, w_out):
    seq_len = xs[0].shape[1]
    s1 = 3 * D_HYENA
    g_row = norm_g.reshape(1, D_MODEL)

    kfull = _filters(hy_w1, hy_b1, hy_w2, hy_b2, hy_w3, hy_b3, hy_w4, hy_freq, seq_len)
    cw = jnp.concatenate([hy_conv_w, hy_conv_b[None]], axis=0)
    cw = jnp.broadcast_to(cw[:, :, None], (4, s1, LANES))
    zrow = jnp.zeros((D_CONF,), F32)
    cf_vec = jnp.stack([cf_dw_b, cf_ln_g, cf_ln_b, cf_pw_b, zrow, zrow, zrow, zrow])
    p_rg, gate, xn, y_cf = _proj_conf(xs, g_row, w_in[:, s1:].astype(BF16), cf_dw_w, cf_vec,
                                      cf_pw_w.astype(BF16))
    u, x0 = _hy_proj(xn, w_in[:, :s1].T.astype(BF16), cw)
    y_hy = _longconv(kfull, u, x0, hy_d)

    n_cb = D_RG // LANES
    zr = jnp.zeros((D_RG,), F32)
    rg_rows = [rg_conv_b, rg_ba[0], rg_bx[0], rg_lam[0], zr, rg_ba[1], rg_bx[1], rg_lam[1]] + [zr] * 8
    rg_vec = jnp.stack(rg_rows).reshape(2 * SUBLANES, n_cb, LANES).transpose(1, 0, 2)
    y_rg = _rglru(p_rg, rg_conv_w, _rg_gate_weights(rg_wa, rg_wx), rg_vec)

    return _out_proj(y_hy, y_cf, y_rg, gate, xs, grp_g.reshape(1, D_MIX), w_out.astype(BF16),
                     final_g.reshape(1, D_MODEL), n_first, split)


def kernel(x_prompt, x_sample, norm_g, w_in, hy_conv_w, hy_conv_b, hy_w1, hy_b1, hy_w2, hy_b2, hy_w3, hy_b3, hy_w4, hy_freq, hy_d, cf_dw_w, cf_dw_b, cf_ln_g, cf_ln_b, cf_pw_w, cf_pw_b, rg_conv_w, rg_conv_b, rg_wa, rg_ba, rg_wx, rg_bx, rg_lam, grp_g, w_out, final_g):
    layer_params = (norm_g, w_in, hy_conv_w, hy_conv_b, hy_w1, hy_b1, hy_w2, hy_b2, hy_w3, hy_b3, hy_w4, hy_freq,
                    hy_d, cf_dw_w, cf_dw_b, cf_ln_g, cf_ln_b, cf_pw_w, cf_pw_b, rg_conv_w, rg_conv_b, rg_wa,
                    rg_ba, rg_wx, rg_bx, rg_lam, grp_g, w_out)
    assert x_prompt.shape[1:] == x_sample.shape[1:]
    n_prompt = x_prompt.shape[0]
    n_rows = (n_prompt + x_sample.shape[0]) * (x_prompt.shape[1] // TOEPLITZ_BLOCK)
    assert n_rows % ROW_TILE == 0, "longconv needs (sequences x time blocks) to fill whole bf16 row tiles"
    depth = norm_g.shape[0]
    xs = (x_prompt, x_sample)
    for l in range(depth):
        xs = tuple(_layer(xs, n_prompt, l == depth - 1, final_g, *[w[l] for w in layer_params]))
    return xs
```

```python
import functools
import math

import jax
import jax.numpy as jnp
from jax import lax
from jax.experimental import pallas as pl
from jax.experimental.pallas import tpu as pltpu

F32 = jnp.float32
BF16 = jnp.bfloat16

D_MODEL = 1024
D_HYENA = 384
D_CONF = 256
D_RG = 384
D_MIX = D_HYENA + D_CONF + D_RG
GROUP_WIDTH = 64
HYENA_BANDS = 16
HYENA_ORDER = 64
HYENA_TARGET = 1e-2
HYENA_FAST_DECAY_PCT = 0.3
HYENA_SLOW_DECAY_PCT = 1.5
CONF_K = 31
RG_C = 8.0
EPS = 1e-6

LANES = 128
SUBLANES = 8
MXU_DIM = 256
VMEM_LIMIT_BYTES = 56 * 1024 * 1024

TOEPLITZ_BLOCK = MXU_DIM
ROW_TILE = 16
FILTER_TILE = 2048
PROJ_ROWS = 512
CONF_HALO = 16
CONF_SUB = 64
OUT_ROWS = 1024
HY_ROWS = 128
RG_CHUNK = 256
RG_PAD = 8


def _params(*sem):
    return pltpu.CompilerParams(dimension_semantics=sem, vmem_limit_bytes=VMEM_LIMIT_BYTES)


def _rms(x, g):
    return x * lax.rsqrt(jnp.mean(x * x, axis=-1, keepdims=True) + EPS) * g


def _sigmoid(x):
    return 0.5 * jnp.tanh(0.5 * x) + 0.5


def _x_tile(x_refs, b, n_first):
    if len(x_refs) == 1:
        return x_refs[0][0]
    return jnp.where(b < n_first, x_refs[0][0], x_refs[1][0])


def _x_specs(xs, rows, seq_tile, hold_tile):
    if len(xs) == 1:
        return [pl.BlockSpec((1, rows, D_MODEL), lambda *g: (*seq_tile(*g), 0))]
    n1 = xs[0].shape[0]
    last = xs[0].shape[1] // rows - 1

    def first(*g):
        b, t = seq_tile(*g)
        return (jnp.minimum(b, n1 - 1), t if hold_tile else jnp.where(b < n1, t, last), 0)

    def second(*g):
        b, t = seq_tile(*g)
        return (jnp.maximum(b - n1, 0), t if hold_tile else jnp.where(b < n1, 0, t), 0)

    return [pl.BlockSpec((1, rows, D_MODEL), first), pl.BlockSpec((1, rows, D_MODEL), second)]


def _filters_kernel(w1t_ref, w1c_ref, w1s_ref, b1_ref, w2_ref, b2_ref, w3_ref, b3_ref, w4_ref,
                    fr_ref, o_ref, *, seq_len):
    i = pl.program_id(0)
    tile = o_ref.shape[1]
    m = i * tile + lax.broadcasted_iota(jnp.int32, (1, tile), 1)
    pos = jnp.abs(m - seq_len).astype(F32)
    t = pos / (seq_len - 1.0)
    w = (2.0 * math.pi / seq_len) * pos
    band = lax.broadcasted_iota(jnp.int32, (HYENA_BANDS, 1), 0).astype(F32)
    f = 1e-4 + band * ((HYENA_BANDS - 1 - 1e-4) / (HYENA_BANDS - 1))
    fw = f * w
    hi = lax.Precision.HIGHEST
    fr = fr_ref[...]
    pre = (w1t_ref[...] * t + jnp.dot(w1c_ref[...], jnp.cos(fw), precision=hi)
           - jnp.dot(w1s_ref[...], jnp.sin(fw), precision=hi))
    h = jnp.sin(fr * (pre + b1_ref[...]))
    h = jnp.sin(fr * (jnp.dot(w2_ref[...], h, precision=hi) + b2_ref[...]))
    h = jnp.sin(fr * (jnp.dot(w3_ref[...], h, precision=hi) + b3_ref[...]))
    k = jnp.dot(w4_ref[...], h, precision=hi)
    max_decay = math.log(HYENA_TARGET) / HYENA_FAST_DECAY_PCT
    min_decay = math.log(HYENA_TARGET) / HYENA_SLOW_DECAY_PCT
    ch = lax.broadcasted_iota(jnp.int32, (D_HYENA, 1), 0).astype(F32)
    delta = jnp.abs(min_decay + ch * ((max_decay - min_decay) / (D_HYENA - 1)))
    decay = jnp.exp(-t * delta)
    k = jnp.where(m >= seq_len, k[:D_HYENA], k[D_HYENA:]) * decay
    o_ref[...] = jnp.where(m == 0, 0.0, k)


def _filters(w1, b1, w2, b2, w3, b3, w4, freq, seq_len):
    w1t = w1.T
    col = lambda v: v.reshape(-1, 1)
    args = (w1t[:, 0:1], w1t[:, 1:1 + HYENA_BANDS], w1t[:, 1 + HYENA_BANDS:], col(b1), w2.T, col(b2),
            w3.T, col(b3), w4.T, col(freq))
    full = lambda a: pl.BlockSpec(a.shape, lambda i: (0, 0))
    return pl.pallas_call(
        functools.partial(_filters_kernel, seq_len=seq_len),
        grid=(2 * seq_len // FILTER_TILE,),
        in_specs=[full(a) for a in args],
        out_specs=pl.BlockSpec((D_HYENA, FILTER_TILE), lambda i: (0, i)),
        out_shape=jax.ShapeDtypeStruct((D_HYENA, 2 * seq_len), F32),
        compiler_params=_params("arbitrary"),
        name="filters",
    )(*args)


def _hy_proj_kernel(xn_ref, w_ref, cw_ref, u_ref, x0_ref, p_ref, l_ref, s_ref):
    j = pl.program_id(1)
    nj = pl.num_programs(1) - 1
    half = LANES

    @pl.when(j == 0)
    def _():
        p_ref[...] = jnp.zeros(p_ref.shape, F32)
        l_ref[...] = jnp.zeros(l_ref.shape, F32)
        s_ref[...] = jnp.zeros(s_ref.shape, F32)

    p_cur = jnp.dot(w_ref[...], xn_ref[0], preferred_element_type=F32)
    s_cur = pltpu.roll(p_cur[:, :half], half - 1, 1)

    lane = lax.broadcasted_iota(jnp.int32, (1, half), 1)
    first = lane == 0
    last = lane == half - 1
    s_right = jnp.where(j < nj, s_cur, 0.0)

    def conv(rows):
        p_lo = p_ref[rows, :half]
        p_hi = p_ref[rows, half:]
        r_lo = pltpu.roll(p_lo, 1, 1)
        r_hi = pltpu.roll(p_hi, 1, 1)
        s_hi = pltpu.roll(p_hi, half - 1, 1)
        w0, w1, w2, bias = (cw_ref[k, rows, :] for k in range(4))
        lo = w0 * jnp.where(first, l_ref[rows, :], r_lo) + w1 * p_lo + w2 * jnp.where(last, s_hi, s_ref[rows, :]) + bias
        hi = w0 * jnp.where(first, r_lo, r_hi) + w1 * p_hi + w2 * jnp.where(last, s_right[rows], s_hi) + bias
        l_ref[rows, :] = r_hi
        return lo, hi

    for r in range(0, D_HYENA, HY_ROWS):
        x0 = conv(slice(r, r + HY_ROWS))
        x1 = conv(slice(D_HYENA + r, D_HYENA + r + HY_ROWS))
        v = conv(slice(2 * D_HYENA + r, 2 * D_HYENA + r + HY_ROWS))
        for h in range(2):
            cols = slice(h * half, (h + 1) * half)
            x0_ref[0, 0, r:r + HY_ROWS, cols] = x0[h]
            u_ref[0, 0, r:r + HY_ROWS, cols] = v[h] * x1[h]
    p_ref[...] = p_cur
    s_ref[...] = s_cur


def _hy_proj(xn, w_hy_t, cw):
    n_seq, _, seq_len = xn.shape
    tb = TOEPLITZ_BLOCK
    nj = seq_len // tb
    out = jax.ShapeDtypeStruct((nj, n_seq, D_HYENA, tb), F32)
    out_spec = pl.BlockSpec((1, 1, D_HYENA, tb), lambda b, j: (jnp.maximum(j - 1, 0), b, 0, 0))
    return pl.pallas_call(
        _hy_proj_kernel,
        grid=(n_seq, nj + 1),
        in_specs=[
            pl.BlockSpec((1, D_MODEL, tb), lambda b, j: (b, 0, jnp.minimum(j, nj - 1))),
            pl.BlockSpec((3 * D_HYENA, D_MODEL), lambda b, j: (0, 0)),
            pl.BlockSpec((4, 3 * D_HYENA, LANES), lambda b, j: (0, 0, 0)),
        ],
        out_specs=[out_spec, out_spec],
        out_shape=[out, out],
        scratch_shapes=[pltpu.VMEM((3 * D_HYENA, tb), F32), pltpu.VMEM((3 * D_HYENA, LANES), F32),
                        pltpu.VMEM((3 * D_HYENA, LANES), F32)],
        compiler_params=_params("arbitrary", "arbitrary"),
        name="hy_proj",
    )(xn, w_hy_t, cw)


def _longconv_kernel(k_ref, u_ref, x0_ref, d_ref, o_ref, ut_ref, yt_ref, s_ref, ub_ref, y_ref, *, n_seq):
    tb = TOEPLITZ_BLOCK
    half = tb // 2
    rows, n_ch, _ = u_ref.shape
    nj = rows // n_seq
    seq_len = nj * tb
    step = math.gcd(n_seq, ROW_TILE)
    n_phase = ROW_TILE // step

    tri = (lax.broadcasted_iota(jnp.int32, (half, half), 1) >= lax.broadcasted_iota(jnp.int32, (half, half), 0))
    s_ref[rows:, :] = jnp.zeros((2 * ROW_TILE, tb), F32)
    ut_ref[...] = pltpu.einshape("rcl->crl", u_ref[...])

    def channel(i, carry):
        def circ(q):
            seg = jnp.broadcast_to(k_ref[i, :, half * q:half * (q + 1)], (half, half))
            return pltpu.roll(seg, 0, 1, stride=1, stride_axis=0)

        s_ref[0:rows, :] = ut_ref[i]
        for p in range(n_phase):
            ub_ref[p] = s_ref[p * step:p * step + rows + ROW_TILE, :].astype(BF16)
        y_ref[...] = jnp.zeros(y_ref.shape, F32)

        q = (seq_len - tb * (nj - 1)) // half
        c_prev = circ(q - 1)
        g_prev = jnp.where(tri, c_prev, circ(q - 2)).astype(BF16)
        for d in range(-(nj - 1), nj):
            c_a = circ(q)
            c_b = circ(q + 1)
            g_a = jnp.where(tri, c_a, c_prev).astype(BF16)
            g_b = jnp.where(tri, c_b, c_a).astype(BF16)
            tile = jnp.concatenate(
                [jnp.concatenate([g_a, g_b], axis=1), jnp.concatenate([g_prev, g_a], axis=1)], axis=0)
            c_prev, g_prev = c_b, g_b
            q += 2
            n = -(-n_seq * (nj - abs(d)) // ROW_TILE) * ROW_TILE
            src = 0 if d >= 0 else n_seq * -d
            dst = n_seq * d if d >= 0 else 0
            ps, pd = (src % ROW_TILE) // step, (dst % ROW_TILE) // step
            src -= ps * step
            dst += ROW_TILE - pd * step
            y_ref[pd, dst:dst + n, :] += jnp.dot(ub_ref[ps, src:src + n, :], tile, preferred_element_type=F32)

        y = y_ref[0, ROW_TILE:ROW_TILE + rows, :]
        for p in range(1, n_phase):
            y = y + y_ref[p, ROW_TILE - p * step:ROW_TILE - p * step + rows, :]
        yt_ref[i] = y
        return carry

    lax.fori_loop(0, n_ch, channel, 0, unroll=2)
    y = pltpu.einshape("crl->rcl", yt_ref[...])
    o_ref[...] = (y + u_ref[...] * d_ref[...][None]) * x0_ref[...]


def _longconv(kfull, u, x0, d_skip):
    nj, n_seq, n_ch, tb = u.shape
    rows = nj * n_seq
    seq_len = nj * tb
    n_phase = ROW_TILE // math.gcd(n_seq, ROW_TILE)
    blk = pl.BlockSpec((rows, SUBLANES, tb), lambda c: (0, c, 0))
    out = pl.pallas_call(
        functools.partial(_longconv_kernel, n_seq=n_seq),
        grid=(n_ch // SUBLANES,),
        in_specs=[
            pl.BlockSpec((SUBLANES, 1, 2 * seq_len), lambda c: (c, 0, 0)),
            blk, blk,
            pl.BlockSpec((SUBLANES, tb), lambda c: (c, 0)),
        ],
        out_specs=blk,
        out_shape=jax.ShapeDtypeStruct((rows, n_ch, tb), F32),
        scratch_shapes=[pltpu.VMEM((SUBLANES, rows, tb), F32), pltpu.VMEM((SUBLANES, rows, tb), F32),
                        pltpu.VMEM((rows + 2 * ROW_TILE, tb), F32), pltpu.VMEM((n_phase, rows + ROW_TILE, tb), BF16),
                        pltpu.VMEM((n_phase, rows + 2 * ROW_TILE, tb), F32)],
        compiler_params=_params("arbitrary"),
        name="longconv",
    )(kfull.reshape(n_ch, 1, 2 * seq_len), u.reshape(rows, n_ch, tb), x0.reshape(rows, n_ch, tb),
      jnp.broadcast_to(d_skip[:, None], (n_ch, tb)))
    return out.reshape(nj, n_seq, n_ch, tb)


def _proj_conf_kernel(*refs, n_x, n_first, nt):
    x_refs = refs[:n_x]
    (g_ref, w_ref, dw_ref, vec_ref, pw_ref, rg_ref, gate_ref, xn_ref, ycf_ref,
     stage_ref, gp_ref, ext_ref, yb_ref, xb_ref) = refs[n_x:]
    s = pl.program_id(0)
    rows = PROJ_ROWS
    half = (CONF_K - 1) // 2
    n_ext = rows + 2 * CONF_HALO

    @pl.when(s == 0)
    def _():
        stage_ref[...] = jnp.zeros(stage_ref.shape, F32)
        gp_ref[...] = jnp.zeros(gp_ref.shape, F32)
        xb_ref[...] = jnp.zeros(xb_ref.shape, BF16)

    p = jnp.dot(xb_ref[...], w_ref[...], preferred_element_type=F32)
    rg_ref[0] = p[:, 2 * D_CONF:2 * D_CONF + D_RG].astype(BF16)
    gate_ref[0] = p[:, 2 * D_CONF + D_RG:].astype(BF16)
    g_cur = p[:, :D_CONF] * _sigmoid(p[:, D_CONF:2 * D_CONF])

    b = jnp.minimum(s, pl.num_programs(0) - 4) // nt
    xn = _rms(_x_tile(x_refs, b, n_first), g_ref[...])
    xn_ref[0] = xn.T.astype(BF16)

    for r in range(1, SUBLANES):
        ext_ref[r, 0:n_ext - SUBLANES, :] = stage_ref[r:r + n_ext - SUBLANES, :]
    vec = vec_ref[...]
    dw_b, ln_g, ln_b, pw_b = vec[0:1], vec[1:2], vec[2:3], vec[3:4]
    dw = dw_ref[...]
    for c in range(rows // CONF_SUB):
        acc = jnp.broadcast_to(dw_b, (CONF_SUB, D_CONF))
        for k in range(CONF_K):
            off = CONF_HALO - half + k
            start = c * CONF_SUB + off - off % SUBLANES
            if off % SUBLANES == 0:
                tap = stage_ref[start:start + CONF_SUB, :]
            else:
                tap = ext_ref[off % SUBLANES, start:start + CONF_SUB, :]
            acc = acc + dw[k:k + 1] * tap
        mu = jnp.mean(acc, axis=-1, keepdims=True)
        xc = acc - mu
        var = jnp.mean(xc * xc, axis=-1, keepdims=True)
        y = xc * lax.rsqrt(var + EPS) * ln_g + ln_b
        yb_ref[c * CONF_SUB:(c + 1) * CONF_SUB, :] = (y * _sigmoid(y)).astype(BF16)
    ycf_ref[0] = (jnp.dot(yb_ref[...], pw_ref[...], preferred_element_type=F32) + pw_b).astype(ycf_ref.dtype)

    t_next = (s + 2 * nt - 2) % nt
    tail = stage_ref[rows:rows + CONF_HALO, :]
    stage_ref[0:CONF_HALO, :] = jnp.where(t_next > 0, tail, 0.0)
    stage_ref[CONF_HALO:CONF_HALO + rows, :] = gp_ref[...]
    stage_ref[CONF_HALO + rows:, :] = jnp.where(t_next < nt - 1, g_cur[0:CONF_HALO], 0.0)
    gp_ref[...] = g_cur
    xb_ref[...] = xn.astype(BF16)


def _proj_conf(xs, norm_g, w_rest, dw_w, vec, pw_w):
    n_seq = sum(x.shape[0] for x in xs)
    seq_len = xs[0].shape[1]
    n_col = w_rest.shape[1]
    nt = seq_len // PROJ_ROWS
    n_tiles = n_seq * nt
    at = lambda lag: lambda s: divmod(jnp.clip(s - lag, 0, n_tiles - 1), nt)
    cur, prev, old = at(0), at(1), at(3)
    row = lambda w, at: pl.BlockSpec((1, PROJ_ROWS, w), lambda s: (*at(s), 0))
    full = lambda a: pl.BlockSpec(a.shape, lambda s: (0, 0))
    return pl.pallas_call(
        functools.partial(_proj_conf_kernel, n_x=len(xs), n_first=xs[0].shape[0], nt=nt),
        grid=(n_tiles + 3,),
        in_specs=_x_specs(xs, PROJ_ROWS, cur, False) + [full(a) for a in (norm_g, w_rest, dw_w, vec, pw_w)],
        out_specs=[row(D_RG, prev), row(D_MIX, prev),
                   pl.BlockSpec((1, D_MODEL, PROJ_ROWS), lambda s: (cur(s)[0], 0, cur(s)[1])),
                   row(D_CONF, old)],
        out_shape=[jax.ShapeDtypeStruct((n_seq, seq_len, D_RG), BF16),
                   jax.ShapeDtypeStruct((n_seq, seq_len, D_MIX), BF16),
                   jax.ShapeDtypeStruct((n_seq, D_MODEL, seq_len), BF16),
                   jax.ShapeDtypeStruct((n_seq, seq_len, D_CONF), BF16)],
        scratch_shapes=[pltpu.VMEM((PROJ_ROWS + 2 * CONF_HALO, D_CONF), F32),
                        pltpu.VMEM((PROJ_ROWS, D_CONF), F32),
                        pltpu.VMEM((SUBLANES, PROJ_ROWS + 2 * CONF_HALO, D_CONF), F32),
                        pltpu.VMEM((PROJ_ROWS, D_CONF), BF16), pltpu.VMEM((PROJ_ROWS, D_MODEL), BF16)],
        compiler_params=_params("arbitrary"),
        name="proj_conf",
    )(*xs, norm_g, w_rest, dw_w, vec, pw_w)


def _shift_sub(x, s, fill, reverse):
    r = x.shape[1]
    sub = lax.broadcasted_iota(jnp.int32, (1, r, 1), 1)
    if reverse:
        return jnp.where(sub < r - s, pltpu.roll(x, r - s, 1), fill)
    return jnp.where(sub >= s, pltpu.roll(x, s, 1), fill)


def _scan_chunk(a, b, carry, reverse):
    rows, lanes = a.shape
    n = rows // SUBLANES
    a = a.reshape(n, SUBLANES, lanes)
    b = b.reshape(n, SUBLANES, lanes)
    s = 1
    while s < SUBLANES:
        b = b + a * _shift_sub(b, s, 0.0, reverse)
        a = a * _shift_sub(a, s, 1.0, reverse)
        s *= 2
    edge = 0 if reverse else SUBLANES - 1
    at = jnp.broadcast_to(a[:, edge:edge + 1, :], a.shape)
    bt = jnp.broadcast_to(b[:, edge:edge + 1, :], b.shape)
    h_in = [None] * n
    for v in (range(n - 1, -1, -1) if reverse else range(n)):
        h_in[v] = carry
        carry = bt[v] + at[v] * carry
    h = b + a * jnp.stack(h_in)
    return h.reshape(rows, lanes), carry


def _rglru_kernel(p_ref, cw_ref, w_ref, vec_ref, o_ref, xp_ref, xr_ref, hf_ref):
    seq_len = p_ref.shape[1]
    lanes = p_ref.shape[2]
    n_chunk = seq_len // RG_CHUNK
    zeros = jnp.zeros((RG_PAD, lanes), F32)
    xp_ref[0:RG_PAD, :] = zeros
    xp_ref[RG_PAD + seq_len:, :] = zeros
    xp_ref[RG_PAD:RG_PAD + seq_len, :] = p_ref[0].astype(F32)
    cw = cw_ref[...]
    vec = vec_ref[0]
    conv_b = vec[0:1]

    def gates(xr, d):
        g = jnp.dot(xr.astype(BF16), w_ref[0, d], preferred_element_type=F32)
        r = _sigmoid(g[:, :lanes] + vec[1 + 4 * d:2 + 4 * d])
        i = _sigmoid(g[:, lanes:] + vec[2 + 4 * d:3 + 4 * d])
        sp = jax.nn.softplus(-vec[3 + 4 * d:4 + 4 * d])
        a = jnp.exp(-RG_C * r * sp)
        m2 = 1.0 - a * a
        return a, jnp.where(m2 == 0.0, 0.0, m2 * lax.rsqrt(m2)) * (i * xr)

    def fwd(c, carry):
        t0 = pl.multiple_of(c * RG_CHUNK, RG_CHUNK)
        n = RG_CHUNK // SUBLANES
        x = xp_ref[pl.ds(t0, RG_CHUNK + 2 * RG_PAD), :].reshape(n + 2, SUBLANES, lanes)
        sub = lax.broadcasted_iota(jnp.int32, (1, SUBLANES, 1), 1)
        xr = conv_b + cw[2:3] * x[1:n + 1].reshape(RG_CHUNK, lanes)
        for k, d in ((0, 2), (1, 1)):
            rolled = pltpu.roll(x[0:n + 1], d, 1)
            tap = jnp.where(sub >= d, rolled[1:], rolled[:n])
            xr = xr + cw[k:k + 1] * tap.reshape(RG_CHUNK, lanes)
        rolled = pltpu.roll(x[1:], SUBLANES - 1, 1)
        tap = jnp.where(sub < SUBLANES - 1, rolled[:n], rolled[1:])
        xr = xr + cw[3:4] * tap.reshape(RG_CHUNK, lanes)
        xr_ref[pl.ds(t0, RG_CHUNK), :] = xr
        a, bt = gates(xr, 0)
        h, carry = _scan_chunk(a, bt, carry, False)
        hf_ref[pl.ds(t0, RG_CHUNK), :] = h
        return carry

    lax.fori_loop(0, n_chunk, fwd, jnp.zeros((SUBLANES, lanes), F32), unroll=2)

    def bwd(i, carry):
        t0 = pl.multiple_of((n_chunk - 1 - i) * RG_CHUNK, RG_CHUNK)
        a, bt = gates(xr_ref[pl.ds(t0, RG_CHUNK), :], 1)
        h, carry = _scan_chunk(a, bt, carry, True)
        o_ref[0, pl.ds(t0, RG_CHUNK), :] = (hf_ref[pl.ds(t0, RG_CHUNK), :] + h).astype(o_ref.dtype)
        return carry

    lax.fori_loop(0, n_chunk, bwd, jnp.zeros((SUBLANES, lanes), F32), unroll=4)


def _rglru(p_rg, conv_w, w_cat, vec):
    n_seq, seq_len, _ = p_rg.shape
    n_cb = D_RG // LANES
    blk = pl.BlockSpec((1, seq_len, LANES), lambda b, c: (b, 0, c))
    return pl.pallas_call(
        _rglru_kernel,
        grid=(n_seq, n_cb),
        in_specs=[
            blk,
            pl.BlockSpec((4, LANES), lambda b, c: (0, c)),
            pl.BlockSpec((1, 2, LANES, 2 * LANES), lambda b, c: (c, 0, 0, 0)),
            pl.BlockSpec((1, 2 * SUBLANES, LANES), lambda b, c: (c, 0, 0)),
        ],
        out_specs=blk,
        out_shape=jax.ShapeDtypeStruct((n_seq, seq_len, D_RG), BF16),
        scratch_shapes=[pltpu.VMEM((seq_len + 2 * RG_PAD, LANES), F32), pltpu.VMEM((seq_len, LANES), F32),
                        pltpu.VMEM((seq_len, LANES), F32)],
        compiler_params=_params("arbitrary", "arbitrary"),
        name="rglru",
    )(p_rg, conv_w, w_cat, vec)


def _out_proj_kernel(yh_ref, ycf_ref, yrg_ref, gate_ref, *refs, n_x, n_first):
    x_refs = refs[:n_x]
    gg_ref, w_ref, fg_ref = refs[n_x:n_x + 3]
    o_refs = refs[n_x + 3:]
    b = pl.program_id(1)
    gg = gg_ref[...]
    yh = jnp.concatenate([yh_ref[k, 0].T for k in range(yh_ref.shape[0])], axis=0)
    y = jnp.concatenate([
        _rms(yh, gg[:, :D_HYENA]),
        _rms(ycf_ref[0].astype(F32), gg[:, D_HYENA:D_HYENA + D_CONF]),
        _rms(yrg_ref[0].astype(F32), gg[:, D_HYENA + D_CONF:]),
    ], axis=-1)
    gate = gate_ref[0].astype(F32)
    y = y * (gate * _sigmoid(gate))
    out = _x_tile(x_refs, b, n_first) + jnp.dot(y.astype(BF16), w_ref[...], preferred_element_type=F32)
    if len(o_refs) == 1:
        o_refs[0][0] = out
        return
    out = _rms(out, fg_ref[...])

    @pl.when(b < n_first)
    def _():
        o_refs[0][0] = out

    @pl.when(b >= n_first)
    def _():
        o_refs[1][0] = out


def _out_proj(y_hy, y_cf, y_rg, gate, xs, grp_g, w_out, final_g, n_first, split):
    nj, n_seq, n_ch, tb = y_hy.shape
    seq_len = nj * tb
    per = OUT_ROWS // tb
    tok = lambda w: pl.BlockSpec((1, OUT_ROWS, w), lambda j, b: (b, j, 0))
    vec = lambda w: pl.BlockSpec((1, w), lambda j, b: (0, 0))
    if not split:
        out_specs = [tok(D_MODEL)]
        out_shape = [jax.ShapeDtypeStruct((n_seq, seq_len, D_MODEL), F32)]
    else:
        out_specs = [
            pl.BlockSpec((1, OUT_ROWS, D_MODEL), lambda j, b: (jnp.minimum(b, n_first - 1), j, 0)),
            pl.BlockSpec((1, OUT_ROWS, D_MODEL), lambda j, b: (jnp.maximum(b - n_first, 0), j, 0)),
        ]
        out_shape = [jax.ShapeDtypeStruct((n, seq_len, D_MODEL), F32) for n in (n_first, n_seq - n_first)]
    return pl.pallas_call(
        functools.partial(_out_proj_kernel, n_x=len(xs), n_first=n_first),
        grid=(nj // per, n_seq),
        in_specs=[
            pl.BlockSpec((per, 1, n_ch, tb), lambda j, b: (j, b, 0, 0)),
            tok(D_CONF), tok(D_RG), tok(D_MIX),
        ] + _x_specs(xs, OUT_ROWS, lambda j, b: (b, j), True) + [
            vec(D_MIX), pl.BlockSpec((D_MIX, D_MODEL), lambda j, b: (0, 0)), vec(D_MODEL),
        ],
        out_specs=out_specs,
        out_shape=out_shape,
        compiler_params=_params("arbitrary", "arbitrary"),
        name="out_proj",
    )(y_hy, y_cf, y_rg, gate, *xs, grp_g, w_out, final_g)


def _rg_gate_weights(wa, wx):
    per = LANES // GROUP_WIDTH
    n_cb = D_RG // LANES

    def diag(w):
        w = w.reshape(n_cb, per, GROUP_WIDTH, GROUP_WIDTH)
        eye = jnp.eye(per, dtype=w.dtype)
        return jnp.einsum("cpij,pq->cpiqj", w, eye).reshape(n_cb, LANES, LANES)

    dirs = [jnp.concatenate([diag(wa[d]), diag(wx[d])], axis=-1) for d in range(2)]
    return jnp.stack(dirs, axis=1).astype(BF16)


def _layer(xs, n_first, split, final_g, norm_g, w_in, hy_conv_w, hy_conv_b, hy_w1, hy_b1, hy_w2, hy_b2, hy_w3,
           hy_b3, hy_w4, hy_freq, hy_d, cf_dw_w, cf_dw_b, cf_ln_g, cf_ln_b, cf_pw_w, cf_pw_b, rg_conv_w,
           rg_conv_b, rg_wa, rg_ba, rg_wx, rg_bx, rg_lam, grp_g, w_out):
    seq_len = xs[0].shape[1]
    s1 = 3 * D_HYENA
    g_row = norm_g.reshape(1, D_MODEL)

    kfull = _filters(hy_w1, hy_b1, hy_w2, hy_b2, hy_w3, hy_b3, hy_w4, hy_freq, seq_len)
    cw = jnp.concatenate([hy_conv_w, hy_conv_b[None]], axis=0)
    cw = jnp.broadcast_to(cw[:, :, None], (4, s1, LANES))
    zrow = jnp.zeros((D_CONF,), F32)
    cf_vec = jnp.stack([cf_dw_b, cf_ln_g, cf_ln_b, cf_pw_b, zrow, zrow, zrow, zrow])
    p_rg, gate, xn, y_cf = _proj_conf(xs, g_row, w_in[:, s1:].astype(BF16), cf_dw_w, cf_vec,
                                      cf_pw_w.astype(BF16))
    u, x0 = _hy_proj(xn, w_in[:, :s1].T.astype(BF16), cw)
    y_hy = _longconv(kfull, u, x0, hy_d)

    n_cb = D_RG // LANES
    zr = jnp.zeros((D_RG,), F32)
    rg_rows = [rg_conv_b, rg_ba[0], rg_bx[0], rg_lam[0], zr, rg_ba[1], rg_bx[1], rg_lam[1]] + [zr] * 8
    rg_vec = jnp.stack(rg_rows).reshape(2 * SUBLANES, n_cb, LANES).transpose(1, 0, 2)
    y_rg = _rglru(p_rg, rg_conv_w, _rg_gate_weights(rg_wa, rg_wx), rg_vec)

    return _out_proj(y_hy, y_cf, y_rg, gate, xs, grp_g.reshape(1, D_MIX), w_out.astype(BF16),
                     final_g.reshape(1, D_MODEL), n_first, split)


def kernel(x_prompt, x_sample, norm_g, w_in, hy_conv_w, hy_conv_b, hy_w1, hy_b1, hy_w2, hy_b2, hy_w3, hy_b3, hy_w4, hy_freq, hy_d, cf_dw_w, cf_dw_b, cf_ln_g, cf_ln_b, cf_pw_w, cf_pw_b, rg_conv_w, rg_conv_b, rg_wa, rg_ba, rg_wx, rg_bx, rg_lam, grp_g, w_out, final_g):
    layer_params = (norm_g, w_in, hy_conv_w, hy_conv_b, hy_w1, hy_b1, hy_w2, hy_b2, hy_w3, hy_b3, hy_w4, hy_freq,
                    hy_d, cf_dw_w, cf_dw_b, cf_ln_g, cf_ln_b, cf_pw_w, cf_pw_b, rg_conv_w, rg_conv_b, rg_wa,
                    rg_ba, rg_wx, rg_bx, rg_lam, grp_g, w_out)
    assert x_prompt.shape[1:] == x_sample.shape[1:]
    n_prompt = x_prompt.shape[0]
    n_rows = (n_prompt + x_sample.shape[0]) * (x_prompt.shape[1] // TOEPLITZ_BLOCK)
    assert n_rows % ROW_TILE == 0, "longconv needs (sequences x time blocks) to fill whole bf16 row tiles"
    depth = norm_g.shape[0]
    xs = (x_prompt, x_sample)
    for l in range(depth):
        xs = tuple(_layer(xs, n_prompt, l == depth - 1, final_g, *[w[l] for w in layer_params]))
    return xs
```
